```python
import math
import jax, jax.numpy as jnp
from jax import lax
import numpy as np

D_MODEL = 2048
BATCH = 1
SEQ = 8192
DEPTH = 4

CHUNK = 64
D_MIX = 2 * D_MODEL
SSD_WIDTH = D_MIX // 2
SBA_WIDTH = D_MIX - SSD_WIDTH
SSD_HEAD_DIM = 64
SSD_HEADS = SSD_WIDTH // SSD_HEAD_DIM
SSD_GROUPS = 4
SSD_STATE = 128
SSD_CONV = 4
SBA_HEAD_DIM = 128
SBA_HEADS = SBA_WIDTH // SBA_HEAD_DIM
Q_BLOCK = 128
EPS = 1e-6
CONV_DIM = SSD_WIDTH + 2 * SSD_GROUPS * SSD_STATE
IN_SPLITS = (
    SSD_WIDTH,
    SSD_WIDTH + CONV_DIM,
    SSD_WIDTH + CONV_DIM + SSD_HEADS,
    SSD_WIDTH + CONV_DIM + SSD_HEADS + SBA_WIDTH,
    SSD_WIDTH + CONV_DIM + SSD_HEADS + 2 * SBA_WIDTH,
    SSD_WIDTH + CONV_DIM + SSD_HEADS + 3 * SBA_WIDTH,
)
IN_COLS = SSD_WIDTH + CONV_DIM + SSD_HEADS + 4 * SBA_WIDTH

kernel_name = "hybrid_ssd_stickbreaking_parallel_heads"


def rmsnorm(x, w):
    xf = x.astype(jnp.float32)
    y = xf * lax.rsqrt(jnp.mean(xf * xf, axis=-1, keepdims=True) + EPS)
    return (y * w.astype(jnp.float32)).astype(x.dtype)


def causal_depthwise_conv(u, w, b):
    k_taps = w.shape[0]
    seq = u.shape[1]
    up = jnp.pad(u, ((0, 0), (k_taps - 1, 0), (0, 0)))
    out = b
    for j in range(k_taps):
        out = out + w[j] * up[:, j:j + seq]
    return out


def ssd_chunked_scan(x, dt, a_head, bm, cm):
    bsz, seq, n_heads, hd = x.shape
    g, n = bm.shape[2], bm.shape[3]
    r = n_heads // g
    nc = seq // CHUNK
    xs = (x.astype(jnp.float32) * dt[..., None]).reshape(bsz, nc, CHUNK, g, r, hd)
    da = (dt * a_head).reshape(bsz, nc, CHUNK, g, r).transpose(0, 3, 4, 1, 2)
    bc = bm.astype(jnp.float32).reshape(bsz, nc, CHUNK, g, n)
    cc = cm.astype(jnp.float32).reshape(bsz, nc, CHUNK, g, n)
    a_cum = jnp.cumsum(da, axis=-1)
    tri = jnp.tril(jnp.ones((CHUNK, CHUNK), dtype=bool))
    seg = a_cum[..., :, None] - a_cum[..., None, :]
    decay_in = jnp.exp(jnp.where(tri, seg, -jnp.inf))
    cb = jnp.einsum('bclgn,bcsgn->bgcls', cc, bc)
    y_diag = jnp.einsum('bgcls,bgrcls,bcsgrp->bclgrp', cb, decay_in, xs)
    decay_states = jnp.exp(a_cum[..., -1:] - a_cum)
    states = jnp.einsum('bclgn,bgrcl,bclgrp->cbgrpn', bc, decay_states, xs)
    chunk_decay = jnp.exp(a_cum[..., -1]).transpose(3, 0, 1, 2)

    def step(h, inp):
        s_c, d_c = inp
        return h * d_c[..., None, None] + s_c, h

    _, prev = lax.scan(step, jnp.zeros(states.shape[1:], jnp.float32), (states, chunk_decay))
    y_off = jnp.einsum('bclgn,cbgrpn,bgrcl->bclgrp', cc, prev, jnp.exp(a_cum))
    return (y_diag + y_off).reshape(bsz, seq, n_heads * hd)


def stick_breaking_attention(q, k, v):
    bsz, n_heads, seq, hd = q.shape
    n_blocks = seq // Q_BLOCK
    scale = 1.0 / math.sqrt(hd)
    kf = k.astype(jnp.float32)
    vf = v.astype(jnp.float32)
    key_pos = jnp.arange(seq)

    def block(i):
        start = i * Q_BLOCK
        qb = lax.dynamic_slice_in_dim(q, start, Q_BLOCK, axis=2).astype(jnp.float32)
        z = jnp.einsum('bhqd,bhkd->bhqk', qb, kf) * scale
        qpos = start + jnp.arange(Q_BLOCK)
        earlier = key_pos[None, :] < qpos[:, None]
        log_keep = jnp.where(earlier, jax.nn.log_sigmoid(-z), 0.0)
        later_sum = lax.cumsum(log_keep, axis=3, reverse=True) - log_keep
        weight = jnp.where(earlier, jnp.exp(jax.nn.log_sigmoid(z) + later_sum), 0.0)
        return jnp.einsum('bhqk,bhkd->bhqd', weight, vf)

    out = lax.map(block, jnp.arange(n_blocks))
    return out.transpose(1, 0, 3, 2, 4).reshape(bsz, seq, n_heads * hd)


def hybrid_layer(x, norm_w, w_in, conv_w, conv_b, dt_bias, a_log, d_skip, ssd_norm_w, w_out):
    bsz, seq, _ = x.shape
    h = rmsnorm(x, norm_w)
    proj = h @ w_in
    z, xbc, dt_raw, q, k, v, g = jnp.split(proj, IN_SPLITS, axis=-1)

    xbc = jax.nn.silu(causal_depthwise_conv(xbc, conv_w, conv_b))
    gn = SSD_GROUPS * SSD_STATE
    xs, bm, cm = jnp.split(xbc, (SSD_WIDTH, SSD_WIDTH + gn), axis=-1)
    xs = xs.reshape(bsz, seq, SSD_HEADS, SSD_HEAD_DIM)
    bm = bm.reshape(bsz, seq, SSD_GROUPS, SSD_STATE)
    cm = cm.reshape(bsz, seq, SSD_GROUPS, SSD_STATE)
    dt = jax.nn.softplus(dt_raw.astype(jnp.float32) + dt_bias.astype(jnp.float32))
    a_head = -jnp.exp(a_log.astype(jnp.float32))
    y = ssd_chunked_scan(xs, dt, a_head, bm, cm)
    y = y + (d_skip.astype(jnp.float32)[:, None] * xs.astype(jnp.float32)).reshape(bsz, seq, SSD_WIDTH)
    yg = (y * jax.nn.silu(z.astype(jnp.float32))).reshape(bsz, seq, SSD_GROUPS, SSD_WIDTH // SSD_GROUPS)
    yg = yg * lax.rsqrt(jnp.mean(yg * yg, axis=-1, keepdims=True) + EPS)
    y_ssd = yg.reshape(bsz, seq, SSD_WIDTH) * ssd_norm_w.astype(jnp.float32)

    def heads(t):
        return t.reshape(bsz, seq, SBA_HEADS, SBA_HEAD_DIM).transpose(0, 2, 1, 3)
    o = stick_breaking_attention(heads(q), heads(k), heads(v))
    y_sba = o * jax.nn.silu(g.astype(jnp.float32))

    mix = jnp.concatenate([y_ssd, y_sba], axis=-1).astype(x.dtype)
    return x + mix @ w_out


def setup_inputs(seed: int = 0) -> dict:
    key = jax.random.key(seed)
    ks = jax.random.split(key, 12)
    f32 = jnp.float32
    x = jax.random.normal(ks[0], (BATCH, SEQ, D_MODEL), f32)
    norm_w = 1.0 + 0.02 * jax.random.normal(ks[1], (DEPTH, D_MODEL), f32)
    w_in = jax.random.normal(ks[2], (DEPTH, D_MODEL, IN_COLS), f32) * D_MODEL ** -0.5
    conv_w = jax.random.normal(ks[3], (DEPTH, SSD_CONV, CONV_DIM), f32) * SSD_CONV ** -0.5
    conv_b = 0.02 * jax.random.normal(ks[4], (DEPTH, CONV_DIM), f32)
    dt0 = jnp.exp(jax.random.uniform(ks[5], (DEPTH, SSD_HEADS), f32,
                                     math.log(1e-3), math.log(1e-1)))
    dt_bias = dt0 + jnp.log(-jnp.expm1(-dt0))
    a_log = jnp.log(jax.random.uniform(ks[6], (DEPTH, SSD_HEADS), f32, 1.0, 16.0))
    d_skip = 1.0 + 0.02 * jax.random.normal(ks[7], (DEPTH, SSD_HEADS), f32)
    ssd_norm_w = 1.0 + 0.02 * jax.random.normal(ks[8], (DEPTH, SSD_WIDTH), f32)
    w_out = jax.random.normal(ks[9], (DEPTH, D_MIX, D_MODEL), f32) * D_MIX ** -0.5
    final_norm_w = 1.0 + 0.02 * jax.random.normal(ks[10], (D_MODEL,), f32)
    return {"x": x, "norm_w": norm_w, "w_in": w_in, "conv_w": conv_w, "conv_b": conv_b,
            "dt_bias": dt_bias, "a_log": a_log, "d_skip": d_skip, "ssd_norm_w": ssd_norm_w,
            "w_out": w_out, "final_norm_w": final_norm_w}


def reference(x, norm_w, w_in, conv_w, conv_b, dt_bias, a_log, d_skip, ssd_norm_w, w_out, final_norm_w):
    h = x
    for i in range(DEPTH):
        h = hybrid_layer(h, norm_w[i], w_in[i], conv_w[i], conv_b[i], dt_bias[i], a_log[i],
                         d_skip[i], ssd_norm_w[i], w_out[i])
    return rmsnorm(h, final_norm_w)
```

```python
import functools
import math

import jax
import jax.numpy as jnp
from jax import lax
from jax.experimental import pallas as pl
from jax.experimental.pallas import tpu as pltpu

F32 = jnp.float32
BF16 = jnp.bfloat16

D_MODEL = 2048
SSD_WIDTH = 2048
SSD_HEAD_DIM = 64
SSD_HEADS = SSD_WIDTH // SSD_HEAD_DIM
SSD_GROUPS = 4
SSD_STATE = 128
SSD_CONV = 4
SSD_BC = SSD_GROUPS * SSD_STATE
CONV_DIM = SSD_WIDTH + 2 * SSD_BC
SSD_COLS = SSD_WIDTH + CONV_DIM
SBA_HEADS = 16
SBA_DIM = 128
SBA_WIDTH = SBA_HEADS * SBA_DIM
EPS = 1e-6

LANES = 128
SUBLANES = 8
VMEM_LIMIT = 56 * 1024 * 1024

SSD_T = 128
SBA_BLK = 128
SBA_QT = 1024
SBA_EXIT = -104.0


def _cparams(sem):
    return pltpu.CompilerParams(dimension_semantics=sem, vmem_limit_bytes=VMEM_LIMIT)


def _sigmoid(a):
    return 1.0 / (1.0 + jnp.exp(-a))


def _softplus(a):
    return jnp.maximum(a, 0.0) + jnp.log(1.0 + jnp.exp(-jnp.abs(a)))


def _split_bf16(a, parts):
    out = []
    r = a
    for i in range(parts):
        p = r.astype(BF16)
        out.append(p)
        if i + 1 < parts:
            r = r - p.astype(F32)
    return out


def _rmsnorm_rows(x, w):
    ms = jnp.mean(x * x, axis=-1, keepdims=True)
    return x * lax.rsqrt(ms + EPS) * w


def _rmsnorm_kernel(x_ref, w_ref, o_ref, *, rows):
    w = w_ref[...]

    def body(r, carry):
        sl = pl.ds(pl.multiple_of(r * rows, rows), rows)
        o_ref[sl, :] = _rmsnorm_rows(x_ref[sl, :], w).astype(o_ref.dtype)
        return carry

    lax.fori_loop(0, x_ref.shape[0] // rows, body, 0)


def rmsnorm_bf16(x, w, tm=256):
    seq, d = x.shape
    return pl.pallas_call(
        functools.partial(_rmsnorm_kernel, rows=32),
        grid=(seq // tm,),
        in_specs=[pl.BlockSpec((tm, d), lambda i: (i, 0)),
                  pl.BlockSpec((1, d), lambda i: (0, 0))],
        out_specs=pl.BlockSpec((tm, d), lambda i: (i, 0)),
        out_shape=jax.ShapeDtypeStruct((seq, d), BF16),
        compiler_params=_cparams(("arbitrary",)),
        name="rmsnorm",
    )(x, w.reshape(1, d))


def _proj_kernel(h_ref, w_ref, o_ref):
    o_ref[...] = jnp.dot(h_ref[...], w_ref[...], preferred_element_type=F32).astype(o_ref.dtype)


def _proj_heads_kernel(h_ref, w_ref, o_ref):
    acc = jnp.dot(h_ref[...], w_ref[...], preferred_element_type=F32)
    for hh in range(o_ref.shape[0]):
        o_ref[hh] = acc[:, hh * LANES:(hh + 1) * LANES].astype(o_ref.dtype)


def in_proj(h, w, tm=512, tn=1024):
    seq, k = h.shape
    n = w.shape[1]
    return pl.pallas_call(
        _proj_kernel,
        grid=(n // tn, seq // tm),
        in_specs=[pl.BlockSpec((tm, k), lambda j, i: (i, 0)),
                  pl.BlockSpec((k, tn), lambda j, i: (0, j))],
        out_specs=pl.BlockSpec((tm, tn), lambda j, i: (i, j)),
        out_shape=jax.ShapeDtypeStruct((seq, n), BF16),
        compiler_params=_cparams(("arbitrary", "arbitrary")),
        name="in_proj_ssd",
    )(h, w)


def in_proj_heads(h, w, tm=512, tn=1024):
    seq, k = h.shape
    n = w.shape[1]
    hp = tn // LANES
    return pl.pallas_call(
        _proj_heads_kernel,
        grid=(n // tn, seq // tm),
        in_specs=[pl.BlockSpec((tm, k), lambda j, i: (i, 0)),
                  pl.BlockSpec((k, tn), lambda j, i: (0, j))],
        out_specs=pl.BlockSpec((hp, tm, LANES), lambda j, i: (j, i, 0)),
        out_shape=jax.ShapeDtypeStruct((n // LANES, seq, LANES), BF16),
        compiler_params=_cparams(("arbitrary", "arbitrary")),
        name="in_proj_sba",
    )(h, w)


def _ssd_kernel(p_ref, h_ref, wdt_ref, cw_ref, cb_ref, dtb_ref, alog_ref, dsk_ref, nw_ref, o_ref,
                ubuf, xbc, state, ybuf):
    t = SSD_T
    hist = SUBLANES
    c = pl.program_id(0)

    @pl.when(c == 0)
    def _():
        ubuf[0:hist, :] = jnp.zeros((hist, CONV_DIM), F32)
        state[...] = jnp.zeros_like(state)

    for j in range(CONV_DIM // LANES):
        cs = slice(j * LANES, (j + 1) * LANES)
        ubuf[hist:hist + t, cs] = p_ref[:, SSD_WIDTH + j * LANES:SSD_WIDTH + (j + 1) * LANES].astype(F32)
    for j in range(CONV_DIM // LANES):
        cs = slice(j * LANES, (j + 1) * LANES)
        acc = cb_ref[:, cs] + cw_ref[0:1, cs] * ubuf[hist - 3:hist - 3 + t, cs]
        for tap in range(1, SSD_CONV):
            acc = acc + cw_ref[tap:tap + 1, cs] * ubuf[hist - 3 + tap:hist - 3 + tap + t, cs]
        xbc[:, cs] = acc * _sigmoid(acc)
    ubuf[0:hist, :] = ubuf[t:t + hist, :]

    dt = _softplus(jnp.dot(h_ref[...], wdt_ref[...], preferred_element_type=F32) + dtb_ref[...])
    a_head = -jnp.exp(alog_ref[...])
    da = dt * a_head
    row = lax.broadcasted_iota(jnp.int32, (t, t), 0)
    col = lax.broadcasted_iota(jnp.int32, (t, t), 1)
    tril = row >= col
    tril_b = jnp.where(tril, 1.0, 0.0).astype(BF16)
    a_cum = None
    for part in _split_bf16(da, 3):
        term = jnp.dot(tril_b, part, preferred_element_type=F32)
        a_cum = term if a_cum is None else a_cum + term
    a_cum_t = a_cum.T
    dt_t = dt.T
    a_last_t = jnp.broadcast_to(a_cum_t[:, t - 1:t], a_cum_t.shape)
    w_t = jnp.exp(a_last_t - a_cum_t) * dt_t
    chunk_decay = jnp.exp(a_cum[t - 1:t, :])
    lane = lax.broadcasted_iota(jnp.int32, (t, LANES), 1)
    low_half = lane < SSD_HEAD_DIM

    heads_per_group = SSD_HEADS // SSD_GROUPS
    for g in range(SSD_GROUPS):
        b_g = xbc[:, SSD_WIDTH + g * SSD_STATE:SSD_WIDTH + (g + 1) * SSD_STATE]
        c_g = xbc[:, SSD_WIDTH + SSD_BC + g * SSD_STATE:SSD_WIDTH + SSD_BC + (g + 1) * SSD_STATE]
        c_gb = c_g.astype(BF16)
        cb = lax.dot_general(c_gb, b_g.astype(BF16), (((1,), (1,)), ((), ())),
                             preferred_element_type=F32)
        b_gt = b_g.T
        gw = heads_per_group * SSD_HEAD_DIM
        y_off = jnp.dot(c_gb, state[:, g * gw:(g + 1) * gw].astype(BF16),
                        preferred_element_type=F32)
        for pr in range(heads_per_group // 2):
            h0 = g * heads_per_group + 2 * pr
            ps = slice(h0 * SSD_HEAD_DIM, (h0 + 2) * SSD_HEAD_DIM)
            x_pair = xbc[:, ps]
            x_lo = jnp.where(low_half, x_pair, 0.0).astype(BF16)
            x_hi = jnp.where(low_half, 0.0, x_pair).astype(BF16)
            y_pair = None
            s_pair = None
            ecols = []
            for hh, x_m in ((h0, x_lo), (h0 + 1, x_hi)):
                colv = jnp.broadcast_to(a_cum[:, hh:hh + 1], (t, t))
                rowv = a_cum_t[hh:hh + 1, :]
                decay = jnp.where(tril, jnp.exp(colv - rowv), 0.0)
                m = (cb * decay * dt_t[hh:hh + 1, :]).astype(BF16)
                yd = jnp.dot(m, x_m, preferred_element_type=F32)
                y_pair = yd if y_pair is None else y_pair + yd
                wm = (b_gt * w_t[hh:hh + 1, :]).astype(BF16)
                sn = jnp.dot(wm, x_m, preferred_element_type=F32)
                s_pair = sn if s_pair is None else s_pair + sn
                ecols.append(jnp.exp(colv))
            e_pair = jnp.where(low_half, ecols[0], ecols[1])
            off = y_off[:, 2 * pr * SSD_HEAD_DIM:(2 * pr + 2) * SSD_HEAD_DIM]
            ybuf[:, ps] = y_pair + e_pair * off + dsk_ref[:, ps] * x_pair
            cd = jnp.where(low_half[0:1, :], chunk_decay[:, h0:h0 + 1], chunk_decay[:, h0 + 1:h0 + 2])
            state[:, ps] = state[:, ps] * cd + s_pair

    gwid = SSD_WIDTH // SSD_GROUPS
    rows = 32
    for g in range(SSD_GROUPS):
        gs = slice(g * gwid, (g + 1) * gwid)
        for r in range(t // rows):
            rs = slice(r * rows, (r + 1) * rows)
            zz = p_ref[rs, gs].astype(F32)
            yg = ybuf[rs, gs] * (zz * _sigmoid(zz))
            ms = jnp.mean(yg * yg, axis=-1, keepdims=True)
            o_ref[rs, gs] = (yg * lax.rsqrt(ms + EPS) * nw_ref[:, gs]).astype(o_ref.dtype)


def ssd_group(p_ssd, h, w_dt, conv_w, conv_b, dt_bias, a_log, d_skip, norm_w):
    seq = p_ssd.shape[0]
    t = SSD_T
    pad = LANES - SSD_HEADS
    wdt_p = jnp.pad(w_dt, ((0, 0), (0, pad))).astype(BF16)
    dtb_p = jnp.pad(dt_bias, (0, pad)).reshape(1, LANES)
    alog_p = jnp.pad(a_log, (0, pad)).reshape(1, LANES)
    dsk = jnp.repeat(d_skip, SSD_HEAD_DIM).reshape(1, SSD_WIDTH)
    full = lambda shape: pl.BlockSpec(shape, lambda i: (0,) * len(shape))
    return pl.pallas_call(
        _ssd_kernel,
        grid=(seq // t,),
        in_specs=[pl.BlockSpec((t, SSD_COLS), lambda i: (i, 0)),
                  pl.BlockSpec((t, D_MODEL), lambda i: (i, 0)),
                  full((D_MODEL, LANES)),
                  full((SSD_CONV, CONV_DIM)),
                  full((1, CONV_DIM)),
                  full((1, LANES)),
                  full((1, LANES)),
                  full((1, SSD_WIDTH)),
                  full((1, SSD_WIDTH))],
        out_specs=pl.BlockSpec((t, SSD_WIDTH), lambda i: (i, 0)),
        out_shape=jax.ShapeDtypeStruct((seq, SSD_WIDTH), BF16),
        scratch_shapes=[pltpu.VMEM((t + 2 * SUBLANES, CONV_DIM), F32),
                        pltpu.VMEM((t, CONV_DIM), F32),
                        pltpu.VMEM((SSD_STATE, SSD_WIDTH), F32),
                        pltpu.VMEM((t, SSD_WIDTH), F32)],
        compiler_params=_cparams(("arbitrary",)),
        name="ssd_group",
    )(p_ssd, h, wdt_p, conv_w, conv_b.reshape(1, CONV_DIM), dtb_p, alog_p, dsk,
      norm_w.reshape(1, SSD_WIDTH))


def _sba_kernel(q_ref, k_ref, v_ref, g_ref, o_ref, c_ref, acc_ref):
    blk = SBA_BLK
    n_sub = q_ref.shape[1] // blk
    qi = pl.program_id(1)
    scale = 1.0 / math.sqrt(SBA_DIM)
    row = lax.broadcasted_iota(jnp.int32, (blk, blk), 0)
    col = lax.broadcasted_iota(jnp.int32, (blk, blk), 1)
    earlier = col < row
    r2 = lax.broadcasted_iota(jnp.int32, (2 * blk, 2 * blk), 0)
    c2 = lax.broadcasted_iota(jnp.int32, (2 * blk, 2 * blk), 1)
    r2 = jnp.where(r2 >= blk, r2 - blk, r2)
    cum = jnp.where((c2 >= blk) | (r2 > c2), 1.0, 0.0).astype(BF16)

    def key_block(q, kb, diag):
        ks = pl.ds(pl.multiple_of(kb * blk, blk), blk)
        z = lax.dot_general(q, k_ref[0, ks, :], (((1,), (1,)), ((), ())),
                            preferred_element_type=F32) * scale
        log_keep = -_softplus(z)
        if diag:
            log_keep = jnp.where(earlier, log_keep, 0.0)
        hi, lo = _split_bf16(log_keep, 2)
        sums = jnp.dot(jnp.concatenate([hi, lo], axis=1), cum, preferred_element_type=F32)
        later = sums[:, :blk]
        c_old = c_ref[...]
        c_new = c_old + sums[:, blk:]
        c_ref[...] = c_new
        w = jnp.exp(z + log_keep + later + c_old)
        if diag:
            w = jnp.where(earlier, w, 0.0)
        acc_ref[...] += jnp.dot(w.astype(BF16), v_ref[0, ks, :], preferred_element_type=F32)
        return jnp.max(c_new)

    def sub_block(sb, carry):
        qs = pl.ds(pl.multiple_of(sb * blk, blk), blk)
        q = q_ref[0, qs, :]
        kb0 = qi * n_sub + sb
        c_ref[...] = jnp.zeros_like(c_ref)
        acc_ref[...] = jnp.zeros_like(acc_ref)
        mx = key_block(q, kb0, True)

        def cond(st):
            kb, m = st
            return jnp.logical_and(kb >= 0, m > SBA_EXIT)

        def body(st):
            kb, _ = st
            return kb - 1, key_block(q, kb, False)

        lax.while_loop(cond, body, (kb0 - 1, mx))
        gg = g_ref[0, qs, :].astype(F32)
        o_ref[qs, :] = (acc_ref[...] * (gg * _sigmoid(gg))).astype(o_ref.dtype)
        return carry

    lax.fori_loop(0, n_sub, sub_block, 0)


def sba_group(p_sba):
    seq = p_sba.shape[1]
    qt = min(SBA_QT, seq)
    nh = SBA_HEADS
    return pl.pallas_call(
        _sba_kernel,
        grid=(nh, seq // qt),
        in_specs=[pl.BlockSpec((1, qt, SBA_DIM), lambda h, i: (h, i, 0)),
                  pl.BlockSpec((1, seq, SBA_DIM), lambda h, i: (nh + h, 0, 0)),
                  pl.BlockSpec((1, seq, SBA_DIM), lambda h, i: (2 * nh + h, 0, 0)),
                  pl.BlockSpec((1, qt, SBA_DIM), lambda h, i: (3 * nh + h, i, 0))],
        out_specs=pl.BlockSpec((qt, SBA_DIM), lambda h, i: (i, h)),
        out_shape=jax.ShapeDtypeStruct((seq, SBA_WIDTH), BF16),
        scratch_shapes=[pltpu.VMEM((SBA_BLK, SBA_BLK), F32),
                        pltpu.VMEM((SBA_BLK, SBA_DIM), F32)],
        compiler_params=_cparams(("arbitrary", "arbitrary")),
        name="sba_group",
    )(p_sba, p_sba, p_sba, p_sba)


def _out_proj_kernel(ya_ref, yb_ref, wa_ref, wb_ref, x_ref, nw_ref, *out_refs, emit_x):
    acc = jnp.dot(ya_ref[...], wa_ref[...], preferred_element_type=F32)
    acc = acc + jnp.dot(yb_ref[...], wb_ref[...], preferred_element_type=F32)
    x_new = x_ref[...] + acc
    if emit_x:
        xo_ref, h_ref = out_refs
        xo_ref[...] = x_new
    else:
        (h_ref,) = out_refs
    h_ref[...] = _rmsnorm_rows(x_new, nw_ref[...]).astype(h_ref.dtype)


def out_proj(y_ssd, y_sba, w_a, w_b, x, next_norm_w, last, tm=256):
    seq, d = x.shape
    ka, kb = w_a.shape[0], w_b.shape[0]
    row_spec = lambda width: pl.BlockSpec((tm, width), lambda i: (i, 0))
    full = lambda shape: pl.BlockSpec(shape, lambda i: (0,) * len(shape))
    if last:
        out_specs = row_spec(d)
        out_shape = jax.ShapeDtypeStruct((seq, d), F32)
    else:
        out_specs = [row_spec(d), row_spec(d)]
        out_shape = [jax.ShapeDtypeStruct((seq, d), F32), jax.ShapeDtypeStruct((seq, d), BF16)]
    return pl.pallas_call(
        functools.partial(_out_proj_kernel, emit_x=not last),
        grid=(seq // tm,),
        in_specs=[row_spec(ka), row_spec(kb), full((ka, d)), full((kb, d)), row_spec(d), full((1, d))],
        out_specs=out_specs,
        out_shape=out_shape,
        compiler_params=_cparams(("arbitrary",)),
        name="out_proj",
    )(y_ssd, y_sba, w_a, w_b, x, next_norm_w.reshape(1, d))


def kernel(x, norm_w, w_in, conv_w, conv_b, dt_bias, a_log, d_skip, ssd_norm_w, w_out, final_norm_w):
    bsz, seq, d = x.shape
    assert bsz == 1 and d == D_MODEL
    depth = w_in.shape[0]
    dt_lo = SSD_COLS
    dt_hi = SSD_COLS + SSD_HEADS
    xs = x.reshape(seq, d)
    h = rmsnorm_bf16(xs, norm_w[0])
    for i in range(depth):
        w_ssd = w_in[i, :, :dt_lo].astype(BF16)
        w_sba = w_in[i, :, dt_hi:].astype(BF16)
        p_ssd = in_proj(h, w_ssd)
        p_sba = in_proj_heads(h, w_sba)
        y_ssd = ssd_group(p_ssd, h, w_in[i, :, dt_lo:dt_hi], conv_w[i], conv_b[i], dt_bias[i],
                          a_log[i], d_skip[i], ssd_norm_w[i])
        y_sba = sba_group(p_sba)
        w_a = w_out[i, :SSD_WIDTH].astype(BF16)
        w_b = w_out[i, SSD_WIDTH:].astype(BF16)
        last = i == depth - 1
        nw = final_norm_w if last else norm_w[i + 1]
        res = out_proj(y_ssd, y_sba, w_a, w_b, xs, nw, last)
        if last:
            out = res
        else:
            xs, h = res
    return out.reshape(bsz, seq, d)
```

```python
import functools
import math

import jax
import jax.numpy as jnp
from jax import lax
from jax.experimental import pallas as pl
from jax.experimental.pallas import tpu as pltpu

F32 = jnp.float32
BF16 = jnp.bfloat16

D_MODEL = 2048
SSD_WIDTH = 2048
SSD_HEAD_DIM = 64
SSD_HEADS = SSD_WIDTH // SSD_HEAD_DIM
SSD_GROUPS = 4
SSD_STATE = 128
SSD_CONV = 4
SSD_BC = SSD_GROUPS * SSD_STATE
CONV_DIM = SSD_WIDTH + 2 * SSD_BC
SSD_COLS = SSD_WIDTH + CONV_DIM
SBA_HEADS = 16
SBA_DIM = 128
SBA_WIDTH = SBA_HEADS * SBA_DIM
EPS = 1e-6

LANES = 128
SUBLANES = 8
VMEM_LIMIT = 56 * 1024 * 1024

SSD_T = 128
SBA_BLK = 128
SBA_QT = 1024
SBA_EXIT = -104.0


def _cparams(sem):
    return pltpu.CompilerParams(dimension_semantics=sem, vmem_limit_bytes=VMEM_LIMIT)


def _sigmoid(a):
    return 1.0 / (1.0 + jnp.exp(-a))


def _softplus(a):
    return jnp.maximum(a, 0.0) + jnp.log(1.0 + jnp.exp(-jnp.abs(a)))


def _split_bf16(a, parts):
    out = []
    r = a
    for i in range(parts):
        p = r.astype(BF16)
        out.append(p)
        if i + 1 < parts:
            r = r - p.astype(F32)
    return out


def _rmsnorm_rows(x, w):
    ms = jnp.mean(x * x, axis=-1, keepdims=True)
    return x * lax.rsqrt(ms + EPS) * w


def _rmsnorm_kernel(x_ref, w_ref, o_ref, *, rows):
    w = w_ref[...]

    def body(r, carry):
        sl = pl.ds(pl.multiple_of(r * rows, rows), rows)
        o_ref[sl, :] = _rmsnorm_rows(x_ref[sl, :], w).astype(o_ref.dtype)
        return carry

    lax.fori_loop(0, x_ref.shape[0] // rows, body, 0)


def rmsnorm_bf16(x, w, tm=256):
    seq, d = x.shape
    return pl.pallas_call(
        functools.partial(_rmsnorm_kernel, rows=32),
        grid=(seq // tm,),
        in_specs=[pl.BlockSpec((tm, d), lambda i: (i, 0)),
                  pl.BlockSpec((1, d), lambda i: (0, 0))],
        out_specs=pl.BlockSpec((tm, d), lambda i: (i, 0)),
        out_shape=jax.ShapeDtypeStruct((seq, d), BF16),
        compiler_params=_cparams(("arbitrary",)),
        name="rmsnorm",
    )(x, w.reshape(1, d))


def _proj_kernel(h_ref, w_ref, o_ref, wb_ref):
    @pl.when(pl.program_id(1) == 0)
    def _():
        rows = 256

        def body(r, carry):
            sl = pl.ds(pl.multiple_of(r * rows, rows), rows)
            wb_ref[sl, :] = w_ref[sl, :].astype(BF16)
            return carry

        lax.fori_loop(0, w_ref.shape[0] // rows, body, 0)

    o_ref[...] = jnp.dot(h_ref[...], wb_ref[...], preferred_element_type=F32).astype(o_ref.dtype)


def _proj_heads_kernel(h_ref, w_ref, o_ref):
    acc = jnp.dot(h_ref[...], w_ref[...], preferred_element_type=F32)
    for hh in range(o_ref.shape[0]):
        o_ref[hh] = acc[:, hh * LANES:(hh + 1) * LANES].astype(o_ref.dtype)


def in_proj(h, w_all, layer, n, tm=512, tn=1024):
    seq, k = h.shape
    return pl.pallas_call(
        _proj_kernel,
        grid=(n // tn, seq // tm),
        in_specs=[pl.BlockSpec((tm, k), lambda j, i: (i, 0)),
                  pl.BlockSpec((None, k, tn), lambda j, i: (layer, 0, j))],
        out_specs=pl.BlockSpec((tm, tn), lambda j, i: (i, j)),
        out_shape=jax.ShapeDtypeStruct((seq, n), BF16),
        scratch_shapes=[pltpu.VMEM((k, tn), BF16)],
        compiler_params=_cparams(("arbitrary", "arbitrary")),
        name="in_proj_ssd",
    )(h, w_all)


def in_proj_heads(h, w_all, layer, tm=512, tn=1024):
    seq, k = h.shape
    n = w_all.shape[2]
    hp = tn // LANES
    return pl.pallas_call(
        _proj_heads_kernel,
        grid=(n // tn, seq // tm),
        in_specs=[pl.BlockSpec((tm, k), lambda j, i: (i, 0)),
                  pl.BlockSpec((None, k, tn), lambda j, i: (layer, 0, j))],
        out_specs=pl.BlockSpec((hp, tm, LANES), lambda j, i: (j, i, 0)),
        out_shape=jax.ShapeDtypeStruct((n // LANES, seq, LANES), BF16),
        compiler_params=_cparams(("arbitrary", "arbitrary")),
        name="in_proj_sba",
    )(h, w_all)


def _ssd_kernel(p_ref, h_ref, wdt_ref, cw_ref, cb_ref, dtb_ref, alog_ref, dsk_ref, nw_ref, o_ref,
                ubuf, xbc, state, ybuf):
    t = SSD_T
    hist = SUBLANES
    c = pl.program_id(0)

    @pl.when(c == 0)
    def _():
        ubuf[0:hist, :] = jnp.zeros((hist, CONV_DIM), F32)
        state[...] = jnp.zeros_like(state)

    for j in range(CONV_DIM // LANES):
        cs = slice(j * LANES, (j + 1) * LANES)
        ubuf[hist:hist + t, cs] = p_ref[:, SSD_WIDTH + j * LANES:SSD_WIDTH + (j + 1) * LANES].astype(F32)
    for j in range(CONV_DIM // LANES):
        cs = slice(j * LANES, (j + 1) * LANES)
        acc = cb_ref[:, cs] + cw_ref[0:1, cs] * ubuf[hist - 3:hist - 3 + t, cs]
        for tap in range(1, SSD_CONV):
            acc = acc + cw_ref[tap:tap + 1, cs] * ubuf[hist - 3 + tap:hist - 3 + tap + t, cs]
        xbc[:, cs] = acc * _sigmoid(acc)
    ubuf[0:hist, :] = ubuf[t:t + hist, :]

    dt = _softplus(jnp.dot(h_ref[...], wdt_ref[...], preferred_element_type=F32) + dtb_ref[...])
    a_head = -jnp.exp(alog_ref[...])
    da = dt * a_head
    row = lax.broadcasted_iota(jnp.int32, (t, t), 0)
    col = lax.broadcasted_iota(jnp.int32, (t, t), 1)
    tril = row >= col
    tril_b = jnp.where(tril, 1.0, 0.0).astype(BF16)
    a_cum = None
    for part in _split_bf16(da, 3):
        term = jnp.dot(tril_b, part, preferred_element_type=F32)
        a_cum = term if a_cum is None else a_cum + term
    a_cum_t = a_cum.T
    dt_t = dt.T
    a_last_t = jnp.broadcast_to(a_cum_t[:, t - 1:t], a_cum_t.shape)
    w_t = jnp.exp(a_last_t - a_cum_t) * dt_t
    chunk_decay = jnp.exp(a_cum[t - 1:t, :])
    lane = lax.broadcasted_iota(jnp.int32, (t, LANES), 1)
    low_half = lane < SSD_HEAD_DIM

    heads_per_group = SSD_HEADS // SSD_GROUPS
    for g in range(SSD_GROUPS):
        b_g = xbc[:, SSD_WIDTH + g * SSD_STATE:SSD_WIDTH + (g + 1) * SSD_STATE]
        c_g = xbc[:, SSD_WIDTH + SSD_BC + g * SSD_STATE:SSD_WIDTH + SSD_BC + (g + 1) * SSD_STATE]
        c_gb = c_g.astype(BF16)
        cb = lax.dot_general(c_gb, b_g.astype(BF16), (((1,), (1,)), ((), ())),
                             preferred_element_type=F32)
        b_gt = b_g.T
        gw = heads_per_group * SSD_HEAD_DIM
        y_off = jnp.dot(c_gb, state[:, g * gw:(g + 1) * gw].astype(BF16),
                        preferred_element_type=F32)
        for pr in range(heads_per_group // 2):
            h0 = g * heads_per_group + 2 * pr
            ps = slice(h0 * SSD_HEAD_DIM, (h0 + 2) * SSD_HEAD_DIM)
            x_pair = xbc[:, ps]
            x_lo = jnp.where(low_half, x_pair, 0.0).astype(BF16)
            x_hi = jnp.where(low_half, 0.0, x_pair).astype(BF16)
            y_pair = None
            s_pair = None
            ecols = []
            for hh, x_m in ((h0, x_lo), (h0 + 1, x_hi)):
                colv = jnp.broadcast_to(a_cum[:, hh:hh + 1], (t, t))
                rowv = a_cum_t[hh:hh + 1, :]
                decay = jnp.where(tril, jnp.exp(colv - rowv), 0.0)
                m = (cb * decay * dt_t[hh:hh + 1, :]).astype(BF16)
                yd = jnp.dot(m, x_m, preferred_element_type=F32)
                y_pair = yd if y_pair is None else y_pair + yd
                wm = (b_gt * w_t[hh:hh + 1, :]).astype(BF16)
                sn = jnp.dot(wm, x_m, preferred_element_type=F32)
                s_pair = sn if s_pair is None else s_pair + sn
                ecols.append(jnp.exp(colv))
            e_pair = jnp.where(low_half, ecols[0], ecols[1])
            off = y_off[:, 2 * pr * SSD_HEAD_DIM:(2 * pr + 2) * SSD_HEAD_DIM]
            ybuf[:, ps] = y_pair + e_pair * off + dsk_ref[:, ps] * x_pair
            cd = jnp.where(low_half[0:1, :], chunk_decay[:, h0:h0 + 1], chunk_decay[:, h0 + 1:h0 + 2])
            state[:, ps] = state[:, ps] * cd + s_pair

    gwid = SSD_WIDTH // SSD_GROUPS
    rows = 32
    for g in range(SSD_GROUPS):
        gs = slice(g * gwid, (g + 1) * gwid)
        for r in range(t // rows):
            rs = slice(r * rows, (r + 1) * rows)
            zz = p_ref[rs, gs].astype(F32)
            yg = ybuf[rs, gs] * (zz * _sigmoid(zz))
            ms = jnp.mean(yg * yg, axis=-1, keepdims=True)
            o_ref[rs, gs] = (yg * lax.rsqrt(ms + EPS) * nw_ref[:, gs]).astype(o_ref.dtype)


def ssd_group(p_ssd, h, layer, wdt, conv_w, conv_b, dtb, alog, dsk, norm_w):
    seq = p_ssd.shape[0]
    t = SSD_T
    per_layer = lambda *shape: pl.BlockSpec((None,) + shape, lambda i: (layer,) + (0,) * len(shape))
    return pl.pallas_call(
        _ssd_kernel,
        grid=(seq // t,),
        in_specs=[pl.BlockSpec((t, SSD_COLS), lambda i: (i, 0)),
                  pl.BlockSpec((t, D_MODEL), lambda i: (i, 0)),
                  per_layer(D_MODEL, LANES),
                  per_layer(SSD_CONV, CONV_DIM),
                  per_layer(1, CONV_DIM),
                  per_layer(1, LANES),
                  per_layer(1, LANES),
                  per_layer(1, SSD_WIDTH),
                  per_layer(1, SSD_WIDTH)],
        out_specs=pl.BlockSpec((t, SSD_WIDTH), lambda i: (i, 0)),
        out_shape=jax.ShapeDtypeStruct((seq, SSD_WIDTH), BF16),
        scratch_shapes=[pltpu.VMEM((t + SUBLANES, CONV_DIM), F32),
                        pltpu.VMEM((t, CONV_DIM), F32),
                        pltpu.VMEM((SSD_STATE, SSD_WIDTH), F32),
                        pltpu.VMEM((t, SSD_WIDTH), F32)],
        compiler_params=_cparams(("arbitrary",)),
        name="ssd_group",
    )(p_ssd, h, wdt, conv_w, conv_b, dtb, alog, dsk, norm_w)


def _sba_kernel(q_ref, k_ref, v_ref, g_ref, o_ref, c_ref, acc_ref):
    blk = SBA_BLK
    n_sub = q_ref.shape[1] // blk
    qi = pl.program_id(1)
    scale = 1.0 / math.sqrt(SBA_DIM)
    row = lax.broadcasted_iota(jnp.int32, (blk, blk), 0)
    col = lax.broadcasted_iota(jnp.int32, (blk, blk), 1)
    earlier = col < row
    r2 = lax.broadcasted_iota(jnp.int32, (2 * blk, 2 * blk), 0)
    c2 = lax.broadcasted_iota(jnp.int32, (2 * blk, 2 * blk), 1)
    r2 = jnp.where(r2 >= blk, r2 - blk, r2)
    cum = jnp.where((c2 >= blk) | (r2 > c2), 1.0, 0.0).astype(BF16)

    def sweep_step(d, diag):
        subs = range(n_sub)
        kbs = [qi * n_sub + sb - d for sb in subs]
        kss = [pl.ds(pl.multiple_of(jnp.maximum(kb, 0) * blk, blk), blk) for kb in kbs]
        zs = [lax.dot_general(q_ref[0, sb * blk:(sb + 1) * blk, :], k_ref[0, kss[sb], :],
                              (((1,), (1,)), ((), ())), preferred_element_type=F32) * scale for sb in subs]
        lks = []
        for sb in subs:
            log_keep = -_softplus(zs[sb])
            lks.append(jnp.where(earlier, log_keep, 0.0) if diag else log_keep)
        sums = []
        for sb in subs:
            hi, lo = _split_bf16(lks[sb], 2)
            sums.append(jnp.dot(jnp.concatenate([hi, lo], axis=1), cum, preferred_element_type=F32))
        pending = None
        ws = []
        for sb in subs:
            c_old = c_ref[sb]
            c_new = c_old + sums[sb][:, blk:]
            c_ref[sb] = c_new
            w = jnp.exp(zs[sb] + lks[sb] + sums[sb][:, :blk] + c_old)
            w = jnp.where(earlier, w, 0.0) if diag else jnp.where(kbs[sb] >= 0, w, 0.0)
            ws.append(w.astype(BF16))
            want = jnp.where(kbs[sb] >= 1, c_new, -jnp.inf)
            pending = want if pending is None else jnp.maximum(pending, want)
        for sb in subs:
            acc_ref[sb] += jnp.dot(ws[sb], v_ref[0, kss[sb], :], preferred_element_type=F32)
        return jnp.max(pending)

    c_ref[...] = jnp.zeros_like(c_ref)
    acc_ref[...] = jnp.zeros_like(acc_ref)
    first = sweep_step(0, True)
    lax.while_loop(lambda st: st[1] > SBA_EXIT,
                   lambda st: (st[0] + 1, sweep_step(st[0], False)),
                   (jnp.int32(1), first))
    for sb in range(n_sub):
        qs = slice(sb * blk, (sb + 1) * blk)
        gg = g_ref[0, qs, :].astype(F32)
        o_ref[qs, :] = (acc_ref[sb] * (gg * _sigmoid(gg))).astype(o_ref.dtype)


def sba_group(p_sba):
    seq = p_sba.shape[1]
    qt = min(SBA_QT, seq)
    nh = SBA_HEADS
    return pl.pallas_call(
        _sba_kernel,
        grid=(nh, seq // qt),
        in_specs=[pl.BlockSpec((1, qt, SBA_DIM), lambda h, i: (h, i, 0)),
                  pl.BlockSpec((1, seq, SBA_DIM), lambda h, i: (nh + h, 0, 0)),
                  pl.BlockSpec((1, seq, SBA_DIM), lambda h, i: (2 * nh + h, 0, 0)),
                  pl.BlockSpec((1, qt, SBA_DIM), lambda h, i: (3 * nh + h, i, 0))],
        out_specs=pl.BlockSpec((qt, SBA_DIM), lambda h, i: (i, h)),
        out_shape=jax.ShapeDtypeStruct((seq, SBA_WIDTH), BF16),
        scratch_shapes=[pltpu.VMEM((qt // SBA_BLK, SBA_BLK, SBA_BLK), F32),
                        pltpu.VMEM((qt // SBA_BLK, SBA_BLK, SBA_DIM), F32)],
        compiler_params=_cparams(("arbitrary", "arbitrary")),
        name="sba_group",
    )(p_sba, p_sba, p_sba, p_sba)


def _out_proj_kernel(ya_ref, yb_ref, wa_ref, wb_ref, x_ref, nw_ref, *out_refs, emit_x):
    acc = jnp.dot(ya_ref[...], wa_ref[...], preferred_element_type=F32)
    acc = acc + jnp.dot(yb_ref[...], wb_ref[...], preferred_element_type=F32)
    x_new = x_ref[...] + acc
    if emit_x:
        xo_ref, h_ref = out_refs
        xo_ref[...] = x_new
    else:
        (h_ref,) = out_refs
    h_ref[...] = _rmsnorm_rows(x_new, nw_ref[...]).astype(h_ref.dtype)


def out_proj(y_ssd, y_sba, w_all, layer, x, norm_all, norm_idx, last, tm=256):
    seq, d = x.shape
    ka, kb = y_ssd.shape[1], y_sba.shape[1]
    assert ka == kb and w_all.shape[1] == ka + kb
    row_spec = lambda width: pl.BlockSpec((tm, width), lambda i: (i, 0))
    if last:
        out_specs = row_spec(d)
        out_shape = jax.ShapeDtypeStruct((seq, d), F32)
    else:
        out_specs = [row_spec(d), row_spec(d)]
        out_shape = [jax.ShapeDtypeStruct((seq, d), F32), jax.ShapeDtypeStruct((seq, d), BF16)]
    return pl.pallas_call(
        functools.partial(_out_proj_kernel, emit_x=not last),
        grid=(seq // tm,),
        in_specs=[row_spec(ka), row_spec(kb),
                  pl.BlockSpec((None, ka, d), lambda i: (layer, 0, 0)),
                  pl.BlockSpec((None, kb, d), lambda i: (layer, 1, 0)),
                  row_spec(d),
                  pl.BlockSpec((None, 1, d), lambda i: (norm_idx, 0, 0))],
        out_specs=out_specs,
        out_shape=out_shape,
        compiler_params=_cparams(("arbitrary",)),
        name="out_proj",
    )(y_ssd, y_sba, w_all, w_all, x, norm_all)


def kernel(x, norm_w, w_in, conv_w, conv_b, dt_bias, a_log, d_skip, ssd_norm_w, w_out, final_norm_w):
    bsz, seq, d = x.shape
    assert bsz == 1 and d == D_MODEL
    depth = w_in.shape[0]
    dt_lo = SSD_COLS
    dt_hi = SSD_COLS + SSD_HEADS
    pad = LANES - SSD_HEADS
    w_sba = w_in[:, :, dt_hi:].astype(BF16)
    w_dt = jnp.pad(w_in[:, :, dt_lo:dt_hi], ((0, 0), (0, 0), (0, pad))).astype(BF16)
    w_out_b = w_out.astype(BF16)
    dtb = jnp.pad(dt_bias, ((0, 0), (0, pad))).reshape(depth, 1, LANES)
    alog = jnp.pad(a_log, ((0, 0), (0, pad))).reshape(depth, 1, LANES)
    dsk = jnp.repeat(d_skip, SSD_HEAD_DIM, axis=1).reshape(depth, 1, SSD_WIDTH)
    conv_b3 = conv_b.reshape(depth, 1, CONV_DIM)
    ssd_nw = ssd_norm_w.reshape(depth, 1, SSD_WIDTH)
    norms = jnp.concatenate([norm_w, final_norm_w[None]], axis=0).reshape(depth + 1, 1, d)

    xs = x.reshape(seq, d)
    h = rmsnorm_bf16(xs, norm_w[0])
    for i in range(depth):
        p_ssd = in_proj(h, w_in, i, SSD_COLS)
        p_sba = in_proj_heads(h, w_sba, i)
        y_ssd = ssd_group(p_ssd, h, i, w_dt, conv_w, conv_b3, dtb, alog, dsk, ssd_nw)
        y_sba = sba_group(p_sba)
        last = i == depth - 1
        res = out_proj(y_ssd, y_sba, w_out_b, i, xs, norms, i + 1, last)
        if last:
            out = res
        else:
            xs, h = res
    return out.reshape(bsz, seq, d)
```

```python
import functools
import math

import jax
import jax.numpy as jnp
from jax import lax
from jax.experimental import pallas as pl
from jax.experimental.pallas import tpu as pltpu

F32 = jnp.float32
BF16 = jnp.bfloat16

D_MODEL = 2048
SSD_WIDTH = 2048
SSD_HEAD_DIM = 64
SSD_HEADS = SSD_WIDTH // SSD_HEAD_DIM
SSD_GROUPS = 4
SSD_STATE = 128
SSD_CONV = 4
SSD_BC = SSD_GROUPS * SSD_STATE
CONV_DIM = SSD_WIDTH + 2 * SSD_BC
SSD_COLS = SSD_WIDTH + CONV_DIM
SBA_HEADS = 16
SBA_DIM = 128
SBA_WIDTH = SBA_HEADS * SBA_DIM
EPS = 1e-6

LANES = 128
SUBLANES = 8
VMEM_LIMIT = 56 * 1024 * 1024

SSD_T = 128
SBA_BLK = 128
SBA_QT = 1024
SBA_EXIT = -104.0


def _cparams(sem):
    return pltpu.CompilerParams(dimension_semantics=sem, vmem_limit_bytes=VMEM_LIMIT)


def _sigmoid(a):
    return 1.0 / (1.0 + jnp.exp(-a))


def _softplus(a):
    return jnp.maximum(a, 0.0) + jnp.log(1.0 + jnp.exp(-jnp.abs(a)))


def _split_bf16(a, parts):
    out = []
    r = a
    for i in range(parts):
        p = r.astype(BF16)
        out.append(p)
        if i + 1 < parts:
            r = r - p.astype(F32)
    return out


def _rmsnorm_rows(x, w):
    ms = jnp.mean(x * x, axis=-1, keepdims=True)
    return x * lax.rsqrt(ms + EPS) * w


def _rmsnorm_kernel(x_ref, w_ref, o_ref, *, rows):
    w = w_ref[...]

    def body(r, carry):
        sl = pl.ds(pl.multiple_of(r * rows, rows), rows)
        o_ref[sl, :] = _rmsnorm_rows(x_ref[sl, :], w).astype(o_ref.dtype)
        return carry

    lax.fori_loop(0, x_ref.shape[0] // rows, body, 0)


def rmsnorm_bf16(x, w, tm=256):
    seq, d = x.shape
    return pl.pallas_call(
        functools.partial(_rmsnorm_kernel, rows=32),
        grid=(seq // tm,),
        in_specs=[pl.BlockSpec((tm, d), lambda i: (i, 0)),
                  pl.BlockSpec((1, d), lambda i: (0, 0))],
        out_specs=pl.BlockSpec((tm, d), lambda i: (i, 0)),
        out_shape=jax.ShapeDtypeStruct((seq, d), BF16),
        compiler_params=_cparams(("arbitrary",)),
        name="rmsnorm",
    )(x, w.reshape(1, d))


def _proj_kernel(h_ref, w_ref, o_ref, wb_ref):
    @pl.when(pl.program_id(1) == 0)
    def _():
        rows = 256

        def body(r, carry):
            sl = pl.ds(pl.multiple_of(r * rows, rows), rows)
            wb_ref[sl, :] = w_ref[sl, :].astype(BF16)
            return carry

        lax.fori_loop(0, w_ref.shape[0] // rows, body, 0)

    o_ref[...] = jnp.dot(h_ref[...], wb_ref[...], preferred_element_type=F32).astype(o_ref.dtype)


def _proj_heads_kernel(h_ref, w_lo_ref, w_hi_ref, o_ref, wb_ref, *, shift):
    @pl.when(pl.program_id(1) == 0)
    def _():
        rows = 128

        def body(r, carry):
            sl = pl.ds(pl.multiple_of(r * rows, rows), rows)
            w = jnp.concatenate([w_lo_ref[sl, shift:], w_hi_ref[sl, :shift]], axis=1)
            wb_ref[sl, :] = w.astype(BF16)
            return carry

        lax.fori_loop(0, wb_ref.shape[0] // rows, body, 0)

    acc = jnp.dot(h_ref[...], wb_ref[...], preferred_element_type=F32)
    for hh in range(o_ref.shape[0]):
        o_ref[hh] = acc[:, hh * LANES:(hh + 1) * LANES].astype(o_ref.dtype)


def in_proj(h, w_all, layer, n, tm=512, tn=1024):
    seq, k = h.shape
    return pl.pallas_call(
        _proj_kernel,
        grid=(n // tn, seq // tm),
        in_specs=[pl.BlockSpec((tm, k), lambda j, i: (i, 0)),
                  pl.BlockSpec((None, k, tn), lambda j, i: (layer, 0, j))],
        out_specs=pl.BlockSpec((tm, tn), lambda j, i: (i, j)),
        out_shape=jax.ShapeDtypeStruct((seq, n), BF16),
        scratch_shapes=[pltpu.VMEM((k, tn), BF16)],
        compiler_params=_cparams(("arbitrary", "arbitrary")),
        name="in_proj_ssd",
    )(h, w_all)


def in_proj_heads(h, w_all, layer, col0, n, tm=512, tn=1024):
    seq, k = h.shape
    hp = tn // LANES
    shift = col0 % LANES
    base = col0 - shift
    assert base % tn == 0 and 0 < shift < LANES and n % tn == 0
    lo0 = base // tn
    hi0 = (base + tn) // LANES
    return pl.pallas_call(
        functools.partial(_proj_heads_kernel, shift=shift),
        grid=(n // tn, seq // tm),
        in_specs=[pl.BlockSpec((tm, k), lambda j, i: (i, 0)),
                  pl.BlockSpec((None, k, tn), lambda j, i: (layer, 0, lo0 + j)),
                  pl.BlockSpec((None, k, LANES), lambda j, i: (layer, 0, hi0 + j * hp))],
        out_specs=pl.BlockSpec((hp, tm, LANES), lambda j, i: (j, i, 0)),
        out_shape=jax.ShapeDtypeStruct((n // LANES, seq, LANES), BF16),
        scratch_shapes=[pltpu.VMEM((k, tn), BF16)],
        compiler_params=_cparams(("arbitrary", "arbitrary")),
        name="in_proj_sba",
    )(h, w_all, w_all)


def _ssd_kernel(p_ref, h_ref, wdt_ref, cw_ref, cb_ref, dtb_ref, alog_ref, dsk_ref, nw_ref, o_ref,
                ubuf, xbc, state, ybuf, wdt_b):
    t = SSD_T
    hist = SUBLANES
    c = pl.program_id(0)

    @pl.when(c == 0)
    def _():
        ubuf[0:hist, :] = jnp.zeros((hist, CONV_DIM), F32)
        state[...] = jnp.zeros_like(state)
        head_lane = lax.broadcasted_iota(jnp.int32, (256, LANES), 1) < SSD_HEADS
        for r in range(D_MODEL // 256):
            rs = slice(r * 256, (r + 1) * 256)
            wdt_b[rs, :] = jnp.where(head_lane, wdt_ref[rs, :], 0.0).astype(BF16)

    for j in range(CONV_DIM // LANES):
        cs = slice(j * LANES, (j + 1) * LANES)
        ubuf[hist:hist + t, cs] = p_ref[:, SSD_WIDTH + j * LANES:SSD_WIDTH + (j + 1) * LANES].astype(F32)
    for j in range(CONV_DIM // LANES):
        cs = slice(j * LANES, (j + 1) * LANES)
        acc = cb_ref[:, cs] + cw_ref[0:1, cs] * ubuf[hist - 3:hist - 3 + t, cs]
        for tap in range(1, SSD_CONV):
            acc = acc + cw_ref[tap:tap + 1, cs] * ubuf[hist - 3 + tap:hist - 3 + tap + t, cs]
        xbc[:, cs] = acc * _sigmoid(acc)
    ubuf[0:hist, :] = ubuf[t:t + hist, :]

    dt = _softplus(jnp.dot(h_ref[...], wdt_b[...], preferred_element_type=F32) + dtb_ref[...])
    a_head = -jnp.exp(alog_ref[...])
    da = dt * a_head
    row = lax.broadcasted_iota(jnp.int32, (t, t), 0)
    col = lax.broadcasted_iota(jnp.int32, (t, t), 1)
    tril = row >= col
    tril_b = jnp.where(tril, 1.0, 0.0).astype(BF16)
    a_cum = None
    for part in _split_bf16(da, 3):
        term = jnp.dot(tril_b, part, preferred_element_type=F32)
        a_cum = term if a_cum is None else a_cum + term
    a_cum_t = a_cum.T
    dt_t = dt.T
    a_last_t = jnp.broadcast_to(a_cum_t[:, t - 1:t], a_cum_t.shape)
    w_t = jnp.exp(a_last_t - a_cum_t) * dt_t
    chunk_decay = jnp.exp(a_cum[t - 1:t, :])
    lane = lax.broadcasted_iota(jnp.int32, (t, LANES), 1)
    low_half = lane < SSD_HEAD_DIM

    heads_per_group = SSD_HEADS // SSD_GROUPS
    for g in range(SSD_GROUPS):
        b_g = xbc[:, SSD_WIDTH + g * SSD_STATE:SSD_WIDTH + (g + 1) * SSD_STATE]
        c_g = xbc[:, SSD_WIDTH + SSD_BC + g * SSD_STATE:SSD_WIDTH + SSD_BC + (g + 1) * SSD_STATE]
        c_gb = c_g.astype(BF16)
        cb = lax.dot_general(c_gb, b_g.astype(BF16), (((1,), (1,)), ((), ())),
                             preferred_element_type=F32)
        b_gt = b_g.T
        gw = heads_per_group * SSD_HEAD_DIM
        y_off = jnp.dot(c_gb, state[:, g * gw:(g + 1) * gw].astype(BF16),
                        preferred_element_type=F32)
        for pr in range(heads_per_group // 2):
            h0 = g * heads_per_group + 2 * pr
            ps = slice(h0 * SSD_HEAD_DIM, (h0 + 2) * SSD_HEAD_DIM)
            x_pair = xbc[:, ps]
            x_lo = jnp.where(low_half, x_pair, 0.0).astype(BF16)
            x_hi = jnp.where(low_half, 0.0, x_pair).astype(BF16)
            y_pair = None
            s_pair = None
            ecols = []
            for hh, x_m in ((h0, x_lo), (h0 + 1, x_hi)):
                colv = jnp.broadcast_to(a_cum[:, hh:hh + 1], (t, t))
                rowv = a_cum_t[hh:hh + 1, :]
                decay = jnp.where(tril, jnp.exp(colv - rowv), 0.0)
                m = (cb * decay * dt_t[hh:hh + 1, :]).astype(BF16)
                yd = jnp.dot(m, x_m, preferred_element_type=F32)
                y_pair = yd if y_pair is None else y_pair + yd
                wm = (b_gt * w_t[hh:hh + 1, :]).astype(BF16)
                sn = jnp.dot(wm, x_m, preferred_element_type=F32)
                s_pair = sn if s_pair is None else s_pair + sn
                ecols.append(jnp.exp(colv))
            e_pair = jnp.where(low_half, ecols[0], ecols[1])
            off = y_off[:, 2 * pr * SSD_HEAD_DIM:(2 * pr + 2) * SSD_HEAD_DIM]
            ybuf[:, ps] = y_pair + e_pair * off + dsk_ref[:, ps] * x_pair
            cd = jnp.where(low_half[0:1, :], chunk_decay[:, h0:h0 + 1], chunk_decay[:, h0 + 1:h0 + 2])
            state[:, ps] = state[:, ps] * cd + s_pair

    gwid = SSD_WIDTH // SSD_GROUPS
    rows = 32
    for g in range(SSD_GROUPS):
        gs = slice(g * gwid, (g + 1) * gwid)
        for r in range(t // rows):
            rs = slice(r * rows, (r + 1) * rows)
            zz = p_ref[rs, gs].astype(F32)
            yg = ybuf[rs, gs] * (zz * _sigmoid(zz))
            ms = jnp.mean(yg * yg, axis=-1, keepdims=True)
            o_ref[rs, gs] = (yg * lax.rsqrt(ms + EPS) * nw_ref[:, gs]).astype(o_ref.dtype)


def ssd_group(p_ssd, h, layer, w_in, conv_w, conv_b, dtb, alog, dsk, norm_w):
    seq = p_ssd.shape[0]
    t = SSD_T
    assert SSD_COLS % LANES == 0
    per_layer = lambda *shape: pl.BlockSpec((None,) + shape, lambda i: (layer,) + (0,) * len(shape))
    return pl.pallas_call(
        _ssd_kernel,
        grid=(seq // t,),
        in_specs=[pl.BlockSpec((t, SSD_COLS), lambda i: (i, 0)),
                  pl.BlockSpec((t, D_MODEL), lambda i: (i, 0)),
                  pl.BlockSpec((None, D_MODEL, LANES), lambda i: (layer, 0, SSD_COLS // LANES)),
                  per_layer(SSD_CONV, CONV_DIM),
                  per_layer(1, CONV_DIM),
                  per_layer(1, LANES),
                  per_layer(1, LANES),
                  per_layer(1, SSD_WIDTH),
                  per_layer(1, SSD_WIDTH)],
        out_specs=pl.BlockSpec((t, SSD_WIDTH), lambda i: (i, 0)),
        out_shape=jax.ShapeDtypeStruct((seq, SSD_WIDTH), BF16),
        scratch_shapes=[pltpu.VMEM((t + SUBLANES, CONV_DIM), F32),
                        pltpu.VMEM((t, CONV_DIM), F32),
                        pltpu.VMEM((SSD_STATE, SSD_WIDTH), F32),
                        pltpu.VMEM((t, SSD_WIDTH), F32),
                        pltpu.VMEM((D_MODEL, LANES), BF16)],
        compiler_params=_cparams(("arbitrary",)),
        name="ssd_group",
    )(p_ssd, h, w_in, conv_w, conv_b, dtb, alog, dsk, norm_w)


def _sba_kernel(q_ref, k_ref, v_ref, g_ref, o_ref, c_ref, acc_ref):
    blk = SBA_BLK
    n_sub = q_ref.shape[1] // blk
    qi = pl.program_id(1)
    scale = 1.0 / math.sqrt(SBA_DIM)
    row = lax.broadcasted_iota(jnp.int32, (blk, blk), 0)
    col = lax.broadcasted_iota(jnp.int32, (blk, blk), 1)
    earlier = col < row
    r2 = lax.broadcasted_iota(jnp.int32, (2 * blk, 2 * blk), 0)
    c2 = lax.broadcasted_iota(jnp.int32, (2 * blk, 2 * blk), 1)
    r2 = jnp.where(r2 >= blk, r2 - blk, r2)
    cum = jnp.where((c2 >= blk) | (r2 > c2), 1.0, 0.0).astype(BF16)

    def sweep_step(d, diag):
        subs = range(n_sub)
        kbs = [qi * n_sub + sb - d for sb in subs]
        kss = [pl.ds(pl.multiple_of(jnp.maximum(kb, 0) * blk, blk), blk) for kb in kbs]
        zs = [lax.dot_general(q_ref[0, sb * blk:(sb + 1) * blk, :], k_ref[0, kss[sb], :],
                              (((1,), (1,)), ((), ())), preferred_element_type=F32) * scale for sb in subs]
        lks = []
        for sb in subs:
            log_keep = -_softplus(zs[sb])
            lks.append(jnp.where(earlier, log_keep, 0.0) if diag else log_keep)
        sums = []
        for sb in subs:
            hi, lo = _split_bf16(lks[sb], 2)
            sums.append(jnp.dot(jnp.concatenate([hi, lo], axis=1), cum, preferred_element_type=F32))
        pending = None
        ws = []
        for sb in subs:
            c_old = c_ref[sb]
            c_new = c_old + sums[sb][:, blk:]
            c_ref[sb] = c_new
            w = jnp.exp(zs[sb] + lks[sb] + sums[sb][:, :blk] + c_old)
            w = jnp.where(earlier, w, 0.0) if diag else jnp.where(kbs[sb] >= 0, w, 0.0)
            ws.append(w.astype(BF16))
            want = jnp.where(kbs[sb] >= 1, c_new, -jnp.inf)
            pending = want if pending is None else jnp.maximum(pending, want)
        for sb in subs:
            acc_ref[sb] += jnp.dot(ws[sb], v_ref[0, kss[sb], :], preferred_element_type=F32)
        return jnp.max(pending)

    c_ref[...] = jnp.zeros_like(c_ref)
    acc_ref[...] = jnp.zeros_like(acc_ref)
    first = sweep_step(0, True)
    lax.while_loop(lambda st: st[1] > SBA_EXIT,
                   lambda st: (st[0] + 1, sweep_step(st[0], False)),
                   (jnp.int32(1), first))
    for sb in range(n_sub):
        qs = slice(sb * blk, (sb + 1) * blk)
        gg = g_ref[0, qs, :].astype(F32)
        o_ref[qs, :] = (acc_ref[sb] * (gg * _sigmoid(gg))).astype(o_ref.dtype)


def sba_group(p_sba):
    seq = p_sba.shape[1]
    qt = min(SBA_QT, seq)
    nh = SBA_HEADS
    return pl.pallas_call(
        _sba_kernel,
        grid=(nh, seq // qt),
        in_specs=[pl.BlockSpec((1, qt, SBA_DIM), lambda h, i: (h, i, 0)),
                  pl.BlockSpec((1, seq, SBA_DIM), lambda h, i: (nh + h, 0, 0)),
                  pl.BlockSpec((1, seq, SBA_DIM), lambda h, i: (2 * nh + h, 0, 0)),
                  pl.BlockSpec((1, qt, SBA_DIM), lambda h, i: (3 * nh + h, i, 0))],
        out_specs=pl.BlockSpec((qt, SBA_DIM), lambda h, i: (i, h)),
        out_shape=jax.ShapeDtypeStruct((seq, SBA_WIDTH), BF16),
        scratch_shapes=[pltpu.VMEM((qt // SBA_BLK, SBA_BLK, SBA_BLK), F32),
                        pltpu.VMEM((qt // SBA_BLK, SBA_BLK, SBA_DIM), F32)],
        compiler_params=_cparams(("arbitrary", "arbitrary")),
        name="sba_group",
    )(p_sba, p_sba, p_sba, p_sba)


def _out_proj_kernel(ya_ref, yb_ref, wa_ref, wb_ref, x_ref, nw_ref, *out_refs, emit_x):
    acc = jnp.dot(ya_ref[...], wa_ref[...], preferred_element_type=F32)
    acc = acc + jnp.dot(yb_ref[...], wb_ref[...], preferred_element_type=F32)
    x_new = x_ref[...] + acc
    if emit_x:
        xo_ref, h_ref = out_refs
        xo_ref[...] = x_new
    else:
        (h_ref,) = out_refs
    h_ref[...] = _rmsnorm_rows(x_new, nw_ref[...]).astype(h_ref.dtype)


def out_proj(y_ssd, y_sba, w_all, layer, x, norm_all, norm_idx, last, tm=256):
    seq, d = x.shape
    ka, kb = y_ssd.shape[1], y_sba.shape[1]
    assert ka == kb and w_all.shape[1] == ka + kb
    row_spec = lambda width: pl.BlockSpec((tm, width), lambda i: (i, 0))
    if last:
        out_specs = row_spec(d)
        out_shape = jax.ShapeDtypeStruct((seq, d), F32)
    else:
        out_specs = [row_spec(d), row_spec(d)]
        out_shape = [jax.ShapeDtypeStruct((seq, d), F32), jax.ShapeDtypeStruct((seq, d), BF16)]
    return pl.pallas_call(
        functools.partial(_out_proj_kernel, emit_x=not last),
        grid=(seq // tm,),
        in_specs=[row_spec(ka), row_spec(kb),
                  pl.BlockSpec((None, ka, d), lambda i: (layer, 0, 0)),
                  pl.BlockSpec((None, kb, d), lambda i: (layer, 1, 0)),
                  row_spec(d),
                  pl.BlockSpec((None, 1, d), lambda i: (norm_idx, 0, 0))],
        out_specs=out_specs,
        out_shape=out_shape,
        compiler_params=_cparams(("arbitrary",)),
        name="out_proj",
    )(y_ssd, y_sba, w_all, w_all, x, norm_all)


def kernel(x, norm_w, w_in, conv_w, conv_b, dt_bias, a_log, d_skip, ssd_norm_w, w_out, final_norm_w):
    bsz, seq, d = x.shape
    assert bsz == 1 and d == D_MODEL
    depth = w_in.shape[0]
    sba_col0 = SSD_COLS + SSD_HEADS
    pad = LANES - SSD_HEADS
    w_out_b = w_out.astype(BF16)
    dtb = jnp.pad(dt_bias, ((0, 0), (0, pad))).reshape(depth, 1, LANES)
    alog = jnp.pad(a_log, ((0, 0), (0, pad))).reshape(depth, 1, LANES)
    dsk = jnp.repeat(d_skip, SSD_HEAD_DIM, axis=1).reshape(depth, 1, SSD_WIDTH)
    conv_b3 = conv_b.reshape(depth, 1, CONV_DIM)
    ssd_nw = ssd_norm_w.reshape(depth, 1, SSD_WIDTH)
    norms = jnp.concatenate([norm_w, final_norm_w[None]], axis=0).reshape(depth + 1, 1, d)

    xs = x.reshape(seq, d)
    h = rmsnorm_bf16(xs, norm_w[0])
    for i in range(depth):
        p_ssd = in_proj(h, w_in, i, SSD_COLS)
        p_sba = in_proj_heads(h, w_in, i, sba_col0, 4 * SBA_WIDTH)
        y_ssd = ssd_group(p_ssd, h, i, w_in, conv_w, conv_b3, dtb, alog, dsk, ssd_nw)
        y_sba = sba_group(p_sba)
        last = i == depth - 1
        res = out_proj(y_ssd, y_sba, w_out_b, i, xs, norms, i + 1, last)
        if last:
            out = res
        else:
            xs, h = res
    return out.reshape(bsz, seq, d)
```

```python
import functools
import math

import jax
import jax.numpy as jnp
from jax import lax
from jax.experimental import pallas as pl
from jax.experimental.pallas import tpu as pltpu

F32 = jnp.float32
BF16 = jnp.bfloat16

D_MODEL = 2048
SSD_WIDTH = 2048
SSD_HEAD_DIM = 64
SSD_HEADS = SSD_WIDTH // SSD_HEAD_DIM
SSD_GROUPS = 4
SSD_STATE = 128
SSD_CONV = 4
SSD_BC = SSD_GROUPS * SSD_STATE
CONV_DIM = SSD_WIDTH + 2 * SSD_BC
SSD_COLS = SSD_WIDTH + CONV_DIM
SBA_HEADS = 16
SBA_DIM = 128
SBA_WIDTH = SBA_HEADS * SBA_DIM
EPS = 1e-6

LANES = 128
SUBLANES = 8
VMEM_LIMIT = 56 * 1024 * 1024

SSD_T = 128
SBA_ROWS = 64
SBA_KEYS = 128
SBA_CUM_GROUP = 2
SBA_QT = 1024
SBA_EXIT = -104.0


def _cparams(sem):
    return pltpu.CompilerParams(dimension_semantics=sem, vmem_limit_bytes=VMEM_LIMIT)


def _sigmoid(a):
    return 1.0 / (1.0 + jnp.exp(-a))


def _softplus(a):
    return jnp.maximum(a, 0.0) + jnp.log(1.0 + jnp.exp(-jnp.abs(a)))


def _split_bf16(a, parts):
    out = []
    r = a
    for i in range(parts):
        p = r.astype(BF16)
        out.append(p)
        if i + 1 < parts:
            r = r - p.astype(F32)
    return out


def _rmsnorm_rows(x, w):
    ms = jnp.mean(x * x, axis=-1, keepdims=True)
    return x * lax.rsqrt(ms + EPS) * w


def _rmsnorm_kernel(x_ref, w_ref, o_ref, *, rows):
    w = w_ref[...]

    def body(r, carry):
        sl = pl.ds(pl.multiple_of(r * rows, rows), rows)
        o_ref[sl, :] = _rmsnorm_rows(x_ref[sl, :], w).astype(o_ref.dtype)
        return carry

    lax.fori_loop(0, x_ref.shape[0] // rows, body, 0)


def rmsnorm_bf16(x, w, tm=256):
    seq, d = x.shape
    return pl.pallas_call(
        functools.partial(_rmsnorm_kernel, rows=32),
        grid=(seq // tm,),
        in_specs=[pl.BlockSpec((tm, d), lambda i: (i, 0)),
                  pl.BlockSpec((1, d), lambda i: (0, 0))],
        out_specs=pl.BlockSpec((tm, d), lambda i: (i, 0)),
        out_shape=jax.ShapeDtypeStruct((seq, d), BF16),
        compiler_params=_cparams(("arbitrary",)),
        name="rmsnorm",
    )(x, w.reshape(1, d))


NT_DIMS = (((1,), (1,)), ((), ()))
W_ROW_ALIGN = 32


def _proj_kernel(h_ref, *refs, shift, heads):
    if shift:
        w_lo_ref, w_hi_ref, o_ref, wb_ref = refs
    else:
        w_lo_ref, o_ref, wb_ref = refs
    tn = wb_ref.shape[0]

    @pl.when(pl.program_id(1) == 0)
    def _():
        rows = W_ROW_ALIGN

        def body(r, carry):
            dst = pl.ds(pl.multiple_of(r * rows, rows), rows)
            src = pl.ds(pl.multiple_of(r * rows + shift, rows), rows)
            wb_ref[dst, :] = w_lo_ref[src, :].astype(BF16)
            return carry

        lax.fori_loop(0, (tn - shift) // rows, body, 0)
        if shift:
            wb_ref[tn - shift:tn, :] = w_hi_ref[...].astype(BF16)

    acc = lax.dot_general(h_ref[...], wb_ref[...], NT_DIMS, preferred_element_type=F32)
    if heads:
        for hh in range(o_ref.shape[0]):
            o_ref[hh] = acc[:, hh * LANES:(hh + 1) * LANES].astype(o_ref.dtype)
    else:
        o_ref[...] = acc.astype(o_ref.dtype)


def in_proj(h, w_t, layer, row0, n, heads, name, tm=512, tn=1024):
    seq, k = h.shape
    shift = row0 % tn
    assert shift % W_ROW_ALIGN == 0 and n % tn == 0 and w_t.shape[1] >= row0 + n
    lo0 = row0 // tn
    in_specs = [pl.BlockSpec((tm, k), lambda j, i: (i, 0)),
                pl.BlockSpec((None, tn, k), lambda j, i: (layer, lo0 + j, 0))]
    operands = [h, w_t]
    if shift:
        assert tn % shift == 0
        per = tn // shift
        in_specs.append(pl.BlockSpec((None, shift, k), lambda j, i: (layer, (lo0 + j + 1) * per, 0)))
        operands.append(w_t)
    if heads:
        hp = tn // LANES
        out_spec = pl.BlockSpec((hp, tm, LANES), lambda j, i: (j, i, 0))
        out_shape = jax.ShapeDtypeStruct((n // LANES, seq, LANES), BF16)
    else:
        out_spec = pl.BlockSpec((tm, tn), lambda j, i: (i, j))
        out_shape = jax.ShapeDtypeStruct((seq, n), BF16)
    return pl.pallas_call(
        functools.partial(_proj_kernel, shift=shift, heads=heads),
        grid=(n // tn, seq // tm),
        in_specs=in_specs,
        out_specs=out_spec,
        out_shape=out_shape,
        scratch_shapes=[pltpu.VMEM((tn, k), BF16)],
        compiler_params=_cparams(("arbitrary", "arbitrary")),
        name=name,
    )(*operands)


def _ssd_kernel(p_ref, h_ref, wdt_ref, cw_ref, cb_ref, dtb_ref, alog_ref, dsk_ref, nw_ref, o_ref,
                ubuf, xbc, state, ybuf, wdt_b):
    t = SSD_T
    hist = SUBLANES
    c = pl.program_id(0)

    @pl.when(c == 0)
    def _():
        ubuf[0:hist, :] = jnp.zeros((hist, CONV_DIM), F32)
        state[...] = jnp.zeros_like(state)
        wdt_b[0:SSD_HEADS, :] = wdt_ref[...].astype(BF16)
        wdt_b[SSD_HEADS:LANES, :] = jnp.zeros((LANES - SSD_HEADS, D_MODEL), BF16)

    for j in range(CONV_DIM // LANES):
        cs = slice(j * LANES, (j + 1) * LANES)
        ubuf[hist:hist + t, cs] = p_ref[:, SSD_WIDTH + j * LANES:SSD_WIDTH + (j + 1) * LANES].astype(F32)
    for j in range(CONV_DIM // LANES):
        cs = slice(j * LANES, (j + 1) * LANES)
        acc = cb_ref[:, cs] + cw_ref[0:1, cs] * ubuf[hist - 3:hist - 3 + t, cs]
        for tap in range(1, SSD_CONV):
            acc = acc + cw_ref[tap:tap + 1, cs] * ubuf[hist - 3 + tap:hist - 3 + tap + t, cs]
        xbc[:, cs] = acc * _sigmoid(acc)
    ubuf[0:hist, :] = ubuf[t:t + hist, :]

    dt = _softplus(lax.dot_general(h_ref[...], wdt_b[...], NT_DIMS, preferred_element_type=F32) + dtb_ref[...])
    a_head = -jnp.exp(alog_ref[...])
    da = dt * a_head
    row = lax.broadcasted_iota(jnp.int32, (t, t), 0)
    col = lax.broadcasted_iota(jnp.int32, (t, t), 1)
    tril = row >= col
    tril_b = jnp.where(tril, 1.0, 0.0).astype(BF16)
    a_cum = None
    for part in _split_bf16(da, 3):
        term = jnp.dot(tril_b, part, preferred_element_type=F32)
        a_cum = term if a_cum is None else a_cum + term
    a_cum_t = a_cum.T
    dt_t = dt.T
    a_last_t = jnp.broadcast_to(a_cum_t[:, t - 1:t], a_cum_t.shape)
    w_t = jnp.exp(a_last_t - a_cum_t) * dt_t
    chunk_decay = jnp.exp(a_cum[t - 1:t, :])
    lane = lax.broadcasted_iota(jnp.int32, (t, LANES), 1)
    low_half = lane < SSD_HEAD_DIM

    heads_per_group = SSD_HEADS // SSD_GROUPS
    for g in range(SSD_GROUPS):
        b_g = xbc[:, SSD_WIDTH + g * SSD_STATE:SSD_WIDTH + (g + 1) * SSD_STATE]
        c_g = xbc[:, SSD_WIDTH + SSD_BC + g * SSD_STATE:SSD_WIDTH + SSD_BC + (g + 1) * SSD_STATE]
        c_gb = c_g.astype(BF16)
        cb = lax.dot_general(c_gb, b_g.astype(BF16), (((1,), (1,)), ((), ())),
                             preferred_element_type=F32)
        b_gt = b_g.T
        gw = heads_per_group * SSD_HEAD_DIM
        y_off = jnp.dot(c_gb, state[:, g * gw:(g + 1) * gw].astype(BF16),
                        preferred_element_type=F32)
        for pr in range(heads_per_group // 2):
            h0 = g * heads_per_group + 2 * pr
            ps = slice(h0 * SSD_HEAD_DIM, (h0 + 2) * SSD_HEAD_DIM)
            x_pair = xbc[:, ps]
            x_lo = jnp.where(low_half, x_pair, 0.0).astype(BF16)
            x_hi = jnp.where(low_half, 0.0, x_pair).astype(BF16)
            y_pair = None
            s_pair = None
            ecols = []
            for hh, x_m in ((h0, x_lo), (h0 + 1, x_hi)):
                colv = jnp.broadcast_to(a_cum[:, hh:hh + 1], (t, t))
                rowv = a_cum_t[hh:hh + 1, :]
                decay = jnp.where(tril, jnp.exp(colv - rowv), 0.0)
                m = (cb * decay * dt_t[hh:hh + 1, :]).astype(BF16)
                yd = jnp.dot(m, x_m, preferred_element_type=F32)
                y_pair = yd if y_pair is None else y_pair + yd
                wm = (b_gt * w_t[hh:hh + 1, :]).astype(BF16)
                sn = jnp.dot(wm, x_m, preferred_element_type=F32)
                s_pair = sn if s_pair is None else s_pair + sn
                ecols.append(jnp.exp(colv))
            e_pair = jnp.where(low_half, ecols[0], ecols[1])
            off = y_off[:, 2 * pr * SSD_HEAD_DIM:(2 * pr + 2) * SSD_HEAD_DIM]
            ybuf[:, ps] = y_pair + e_pair * off + dsk_ref[:, ps] * x_pair
            cd = jnp.where(low_half[0:1, :], chunk_decay[:, h0:h0 + 1], chunk_decay[:, h0 + 1:h0 + 2])
            state[:, ps] = state[:, ps] * cd + s_pair

    gwid = SSD_WIDTH // SSD_GROUPS
    rows = 32
    for g in range(SSD_GROUPS):
        gs = slice(g * gwid, (g + 1) * gwid)
        for r in range(t // rows):
            rs = slice(r * rows, (r + 1) * rows)
            zz = p_ref[rs, gs].astype(F32)
            yg = ybuf[rs, gs] * (zz * _sigmoid(zz))
            ms = jnp.mean(yg * yg, axis=-1, keepdims=True)
            o_ref[rs, gs] = (yg * lax.rsqrt(ms + EPS) * nw_ref[:, gs]).astype(o_ref.dtype)


def ssd_group(p_ssd, h, layer, w_t, conv_w, conv_b, dtb, alog, dsk, norm_w):
    seq = p_ssd.shape[0]
    t = SSD_T
    assert SSD_COLS % SSD_HEADS == 0
    per_layer = lambda *shape: pl.BlockSpec((None,) + shape, lambda i: (layer,) + (0,) * len(shape))
    return pl.pallas_call(
        _ssd_kernel,
        grid=(seq // t,),
        in_specs=[pl.BlockSpec((t, SSD_COLS), lambda i: (i, 0)),
                  pl.BlockSpec((t, D_MODEL), lambda i: (i, 0)),
                  pl.BlockSpec((None, SSD_HEADS, D_MODEL), lambda i: (layer, SSD_COLS // SSD_HEADS, 0)),
                  per_layer(SSD_CONV, CONV_DIM),
                  per_layer(1, CONV_DIM),
                  per_layer(1, LANES),
                  per_layer(1, LANES),
                  per_layer(1, SSD_WIDTH),
                  per_layer(1, SSD_WIDTH)],
        out_specs=pl.BlockSpec((t, SSD_WIDTH), lambda i: (i, 0)),
        out_shape=jax.ShapeDtypeStruct((seq, SSD_WIDTH), BF16),
        scratch_shapes=[pltpu.VMEM((t + SUBLANES, CONV_DIM), F32),
                        pltpu.VMEM((t, CONV_DIM), F32),
                        pltpu.VMEM((SSD_STATE, SSD_WIDTH), F32),
                        pltpu.VMEM((t, SSD_WIDTH), F32),
                        pltpu.VMEM((LANES, D_MODEL), BF16)],
        compiler_params=_cparams(("arbitrary",)),
        name="ssd_group",
    )(p_ssd, h, w_t, conv_w, conv_b, dtb, alog, dsk, norm_w)


def _sba_kernel(q_ref, k_ref, v_ref, g_ref, o_ref, c_ref, acc_ref):
    rows = SBA_ROWS
    kw = SBA_KEYS
    qt = q_ref.shape[1]
    n_sub = qt // rows
    subs = range(n_sub)
    qi = pl.program_id(1)
    scale = 1.0 / math.sqrt(SBA_DIM)
    row = lax.broadcasted_iota(jnp.int32, (rows, kw), 0)
    col = lax.broadcasted_iota(jnp.int32, (rows, kw), 1)
    col_minus_row = col - row
    r2 = lax.broadcasted_iota(jnp.int32, (2 * kw, 2 * kw), 0)
    c2 = lax.broadcasted_iota(jnp.int32, (2 * kw, 2 * kw), 1)
    r2 = jnp.where(r2 >= kw, r2 - kw, r2)
    neg_cum = jnp.where((c2 >= kw) | (r2 > c2), -1.0, 0.0).astype(BF16)

    def sweep_step(d, diag):
        q0s = [qi * qt + s * rows for s in subs]
        starts = [q0 + rows - (d + 1) * kw for q0 in q0s]
        loads = [jnp.maximum(ws, 0) for ws in starts]
        kss = [pl.ds(pl.multiple_of(ld, rows), kw) for ld in loads]
        zs = [lax.dot_general(q_ref[0, s * rows:(s + 1) * rows, :], k_ref[0, kss[s], :], NT_DIMS,
                              preferred_element_type=F32) * scale for s in subs]
        if diag:
            masks = [col_minus_row < (q0s[s] - loads[s]) for s in subs]
        else:
            masks = [col < (starts[s] + kw - loads[s]) for s in subs]
        sps = [jnp.where(masks[s], _softplus(zs[s]), 0.0) for s in subs]
        sums = []
        for g0 in range(0, n_sub, SBA_CUM_GROUP):
            parts = []
            for s in range(g0, g0 + SBA_CUM_GROUP):
                hi, lo = _split_bf16(sps[s], 2)
                parts.append(jnp.concatenate([hi, lo], axis=1))
            grp = jnp.dot(jnp.concatenate(parts, axis=0), neg_cum, preferred_element_type=F32)
            sums.extend(grp[i * rows:(i + 1) * rows] for i in range(SBA_CUM_GROUP))
        pending = None
        ws_b = []
        for s in subs:
            later = sums[s][:, :kw]
            c_old = c_ref[s]
            c_new = c_old + sums[s][:, kw:]
            c_ref[s] = c_new
            w = jnp.exp(zs[s] - sps[s] + later + c_old)
            ws_b.append(jnp.where(masks[s], w, 0.0).astype(BF16))
            want = jnp.where(starts[s] > 0, c_new, -jnp.inf)
            pending = want if pending is None else jnp.maximum(pending, want)
        for s in subs:
            acc_ref[s] += jnp.dot(ws_b[s], v_ref[0, kss[s], :], preferred_element_type=F32)
        return jnp.max(pending)

    c_ref[...] = jnp.zeros_like(c_ref)
    acc_ref[...] = jnp.zeros_like(acc_ref)
    first = sweep_step(0, True)
    lax.while_loop(lambda st: st[1] > SBA_EXIT,
                   lambda st: (st[0] + 1, sweep_step(st[0], False)),
                   (jnp.int32(1), first))
    for s in subs:
        qs = slice(s * rows, (s + 1) * rows)
        gg = g_ref[0, qs, :].astype(F32)
        o_ref[qs, :] = (acc_ref[s] * (gg * _sigmoid(gg))).astype(o_ref.dtype)


def sba_group(p_sba):
    seq = p_sba.shape[1]
    qt = min(SBA_QT, seq)
    nh = SBA_HEADS
    return pl.pallas_call(
        _sba_kernel,
        grid=(nh, seq // qt),
        in_specs=[pl.BlockSpec((1, qt, SBA_DIM), lambda h, i: (h, i, 0)),
                  pl.BlockSpec((1, seq, SBA_DIM), lambda h, i: (nh + h, 0, 0)),
                  pl.BlockSpec((1, seq, SBA_DIM), lambda h, i: (2 * nh + h, 0, 0)),
                  pl.BlockSpec((1, qt, SBA_DIM), lambda h, i: (3 * nh + h, i, 0))],
        out_specs=pl.BlockSpec((qt, SBA_DIM), lambda h, i: (i, h)),
        out_shape=jax.ShapeDtypeStruct((seq, SBA_WIDTH), BF16),
        scratch_shapes=[pltpu.VMEM((qt // SBA_ROWS, SBA_ROWS, SBA_KEYS), F32),
                        pltpu.VMEM((qt // SBA_ROWS, SBA_ROWS, SBA_DIM), F32)],
        compiler_params=_cparams(("arbitrary", "arbitrary")),
        name="sba_group",
    )(p_sba, p_sba, p_sba, p_sba)


def _out_proj_kernel(ya_ref, yb_ref, wa_ref, wb_ref, x_ref, nw_ref, *out_refs, emit_x):
    acc = jnp.dot(ya_ref[...], wa_ref[...], preferred_element_type=F32)
    acc = acc + jnp.dot(yb_ref[...], wb_ref[...], preferred_element_type=F32)
    x_new = x_ref[...] + acc
    if emit_x:
        xo_ref, h_ref = out_refs
        xo_ref[...] = x_new
    else:
        (h_ref,) = out_refs
    h_ref[...] = _rmsnorm_rows(x_new, nw_ref[...]).astype(h_ref.dtype)


def out_proj(y_ssd, y_sba, w_all, layer, x, norm_all, norm_idx, last, tm=256):
    seq, d = x.shape
    ka, kb = y_ssd.shape[1], y_sba.shape[1]
    assert ka == kb and w_all.shape[1] == ka + kb
    row_spec = lambda width: pl.BlockSpec((tm, width), lambda i: (i, 0))
    if last:
        out_specs = row_spec(d)
        out_shape = jax.ShapeDtypeStruct((seq, d), F32)
    else:
        out_specs = [row_spec(d), row_spec(d)]
        out_shape = [jax.ShapeDtypeStruct((seq, d), F32), jax.ShapeDtypeStruct((seq, d), BF16)]
    return pl.pallas_call(
        functools.partial(_out_proj_kernel, emit_x=not last),
        grid=(seq // tm,),
        in_specs=[row_spec(ka), row_spec(kb),
                  pl.BlockSpec((None, ka, d), lambda i: (layer, 0, 0)),
                  pl.BlockSpec((None, kb, d), lambda i: (layer, 1, 0)),
                  row_spec(d),
                  pl.BlockSpec((None, 1, d), lambda i: (norm_idx, 0, 0))],
        out_specs=out_specs,
        out_shape=out_shape,
        compiler_params=_cparams(("arbitrary",)),
        name="out_proj",
    )(y_ssd, y_sba, w_all, w_all, x, norm_all)


def kernel(x, norm_w, w_in, conv_w, conv_b, dt_bias, a_log, d_skip, ssd_norm_w, w_out, final_norm_w):
    bsz, seq, d = x.shape
    assert bsz == 1 and d == D_MODEL
    depth = w_in.shape[0]
    sba_col0 = SSD_COLS + SSD_HEADS
    pad = LANES - SSD_HEADS
    w_in_t = jnp.swapaxes(w_in, 1, 2)
    w_out_b = w_out.astype(BF16)
    dtb = jnp.pad(dt_bias, ((0, 0), (0, pad))).reshape(depth, 1, LANES)
    alog = jnp.pad(a_log, ((0, 0), (0, pad))).reshape(depth, 1, LANES)
    dsk = jnp.repeat(d_skip, SSD_HEAD_DIM, axis=1).reshape(depth, 1, SSD_WIDTH)
    conv_b3 = conv_b.reshape(depth, 1, CONV_DIM)
    ssd_nw = ssd_norm_w.reshape(depth, 1, SSD_WIDTH)
    norms = jnp.concatenate([norm_w, final_norm_w[None]], axis=0).reshape(depth + 1, 1, d)

    xs = x.reshape(seq, d)
    h = rmsnorm_bf16(xs, norm_w[0])
    for i in range(depth):
        p_ssd = in_proj(h, w_in_t, i, 0, SSD_COLS, False, "in_proj_ssd")
        p_sba = in_proj(h, w_in_t, i, sba_col0, 4 * SBA_WIDTH, True, "in_proj_sba")
        y_ssd = ssd_group(p_ssd, h, i, w_in_t, conv_w, conv_b3, dtb, alog, dsk, ssd_nw)
        y_sba = sba_group(p_sba)
        last = i == depth - 1
        res = out_proj(y_ssd, y_sba, w_out_b, i, xs, norms, i + 1, last)
        if last:
            out = res
        else:
            xs, h = res
    return out.reshape(bsz, seq, d)
```

```python
import functools
import math

import jax
import jax.numpy as jnp
from jax import lax
from jax.experimental import pallas as pl
from jax.experimental.pallas import tpu as pltpu

F32 = jnp.float32
BF16 = jnp.bfloat16

D_MODEL = 2048
SSD_WIDTH = 2048
SSD_HEAD_DIM = 64
SSD_HEADS = SSD_WIDTH // SSD_HEAD_DIM
SSD_GROUPS = 4
SSD_STATE = 128
SSD_CONV = 4
SSD_BC = SSD_GROUPS * SSD_STATE
CONV_DIM = SSD_WIDTH + 2 * SSD_BC
SSD_COLS = SSD_WIDTH + CONV_DIM
SBA_HEADS = 16
SBA_DIM = 128
SBA_WIDTH = SBA_HEADS * SBA_DIM
EPS = 1e-6

LANES = 128
SUBLANES = 8
VMEM_LIMIT = 56 * 1024 * 1024

SSD_T = 128
SBA_ROWS = 64
SBA_KEYS = 128
SBA_CUM_GROUP = 2
SBA_QT = 2048
SBA_EXIT = -104.0


def _cparams(sem):
    return pltpu.CompilerParams(dimension_semantics=sem, vmem_limit_bytes=VMEM_LIMIT)


def _sigmoid(a):
    return 1.0 / (1.0 + jnp.exp(-a))


def _softplus(a):
    return jnp.maximum(a, 0.0) + jnp.log(1.0 + jnp.exp(-jnp.abs(a)))


def _split_bf16(a, parts):
    out = []
    r = a
    for i in range(parts):
        p = r.astype(BF16)
        out.append(p)
        if i + 1 < parts:
            r = r - p.astype(F32)
    return out


def _rmsnorm_rows(x, w):
    ms = jnp.mean(x * x, axis=-1, keepdims=True)
    return x * lax.rsqrt(ms + EPS) * w


def _rmsnorm_kernel(x_ref, w_ref, o_ref, *, rows):
    w = w_ref[...]

    def body(r, carry):
        sl = pl.ds(pl.multiple_of(r * rows, rows), rows)
        o_ref[sl, :] = _rmsnorm_rows(x_ref[sl, :], w).astype(o_ref.dtype)
        return carry

    lax.fori_loop(0, x_ref.shape[0] // rows, body, 0)


def rmsnorm_bf16(x, w, tm=256):
    seq, d = x.shape
    return pl.pallas_call(
        functools.partial(_rmsnorm_kernel, rows=32),
        grid=(seq // tm,),
        in_specs=[pl.BlockSpec((tm, d), lambda i: (i, 0)),
                  pl.BlockSpec((1, d), lambda i: (0, 0))],
        out_specs=pl.BlockSpec((tm, d), lambda i: (i, 0)),
        out_shape=jax.ShapeDtypeStruct((seq, d), BF16),
        compiler_params=_cparams(("arbitrary",)),
        name="rmsnorm",
    )(x, w.reshape(1, d))


NT_DIMS = (((1,), (1,)), ((), ()))
W_ROW_ALIGN = 32


def _proj_kernel(h_ref, *refs, shift, heads):
    if shift:
        w_lo_ref, w_hi_ref, o_ref, wb_ref = refs
    else:
        w_lo_ref, o_ref, wb_ref = refs
    tn = wb_ref.shape[0]

    @pl.when(pl.program_id(1) == 0)
    def _():
        rows = W_ROW_ALIGN

        def body(r, carry):
            dst = pl.ds(pl.multiple_of(r * rows, rows), rows)
            src = pl.ds(pl.multiple_of(r * rows + shift, rows), rows)
            wb_ref[dst, :] = w_lo_ref[src, :].astype(BF16)
            return carry

        lax.fori_loop(0, (tn - shift) // rows, body, 0)
        if shift:
            wb_ref[tn - shift:tn, :] = w_hi_ref[...].astype(BF16)

    acc = lax.dot_general(h_ref[...], wb_ref[...], NT_DIMS, preferred_element_type=F32)
    if heads:
        for hh in range(o_ref.shape[0]):
            o_ref[hh] = acc[:, hh * LANES:(hh + 1) * LANES].astype(o_ref.dtype)
    else:
        o_ref[...] = acc.astype(o_ref.dtype)


def in_proj(h, w_t, layer, row0, n, heads, name, tm=1024, tn=1024):
    seq, k = h.shape
    shift = row0 % tn
    assert shift % W_ROW_ALIGN == 0 and n % tn == 0 and w_t.shape[1] >= row0 + n
    lo0 = row0 // tn
    in_specs = [pl.BlockSpec((tm, k), lambda j, i: (i, 0)),
                pl.BlockSpec((None, tn, k), lambda j, i: (layer, lo0 + j, 0))]
    operands = [h, w_t]
    if shift:
        assert tn % shift == 0
        per = tn // shift
        in_specs.append(pl.BlockSpec((None, shift, k), lambda j, i: (layer, (lo0 + j + 1) * per, 0)))
        operands.append(w_t)
    if heads:
        hp = tn // LANES
        out_spec = pl.BlockSpec((hp, tm, LANES), lambda j, i: (j, i, 0))
        out_shape = jax.ShapeDtypeStruct((n // LANES, seq, LANES), BF16)
    else:
        out_spec = pl.BlockSpec((tm, tn), lambda j, i: (i, j))
        out_shape = jax.ShapeDtypeStruct((seq, n), BF16)
    return pl.pallas_call(
        functools.partial(_proj_kernel, shift=shift, heads=heads),
        grid=(n // tn, seq // tm),
        in_specs=in_specs,
        out_specs=out_spec,
        out_shape=out_shape,
        scratch_shapes=[pltpu.VMEM((tn, k), BF16)],
        compiler_params=_cparams(("arbitrary", "arbitrary")),
        name=name,
    )(*operands)


def _ssd_kernel(p_ref, h_ref, wdt_ref, cw_ref, cb_ref, dtb_ref, alog_ref, dsk_ref, nw_ref, o_ref,
                ubuf, xbc, state, ybuf, wdt_b):
    t = SSD_T
    hist = SUBLANES
    c = pl.program_id(0)

    @pl.when(c == 0)
    def _():
        ubuf[0:hist, :] = jnp.zeros((hist, CONV_DIM), F32)
        state[...] = jnp.zeros_like(state)
        wdt_b[0:SSD_HEADS, :] = wdt_ref[...].astype(BF16)
        wdt_b[SSD_HEADS:LANES, :] = jnp.zeros((LANES - SSD_HEADS, D_MODEL), BF16)

    for j in range(CONV_DIM // LANES):
        cs = slice(j * LANES, (j + 1) * LANES)
        ubuf[hist:hist + t, cs] = p_ref[:, SSD_WIDTH + j * LANES:SSD_WIDTH + (j + 1) * LANES].astype(F32)
    for j in range(CONV_DIM // LANES):
        cs = slice(j * LANES, (j + 1) * LANES)
        acc = cb_ref[:, cs] + cw_ref[0:1, cs] * ubuf[hist - 3:hist - 3 + t, cs]
        for tap in range(1, SSD_CONV):
            acc = acc + cw_ref[tap:tap + 1, cs] * ubuf[hist - 3 + tap:hist - 3 + tap + t, cs]
        xbc[:, cs] = acc * _sigmoid(acc)
    ubuf[0:hist, :] = ubuf[t:t + hist, :]

    dt = _softplus(lax.dot_general(h_ref[...], wdt_b[...], NT_DIMS, preferred_element_type=F32) + dtb_ref[...])
    a_head = -jnp.exp(alog_ref[...])
    da = dt * a_head
    row = lax.broadcasted_iota(jnp.int32, (t, t), 0)
    col = lax.broadcasted_iota(jnp.int32, (t, t), 1)
    tril = row >= col
    tril_b = jnp.where(tril, 1.0, 0.0).astype(BF16)
    a_cum = None
    for part in _split_bf16(da, 3):
        term = jnp.dot(tril_b, part, preferred_element_type=F32)
        a_cum = term if a_cum is None else a_cum + term
    a_cum_t = a_cum.T
    dt_t = dt.T
    a_last_t = jnp.broadcast_to(a_cum_t[:, t - 1:t], a_cum_t.shape)
    w_t = jnp.exp(a_last_t - a_cum_t) * dt_t
    chunk_decay = jnp.exp(a_cum[t - 1:t, :])
    lane = lax.broadcasted_iota(jnp.int32, (t, LANES), 1)
    low_half = lane < SSD_HEAD_DIM

    heads_per_group = SSD_HEADS // SSD_GROUPS
    for g in range(SSD_GROUPS):
        b_g = xbc[:, SSD_WIDTH + g * SSD_STATE:SSD_WIDTH + (g + 1) * SSD_STATE]
        c_g = xbc[:, SSD_WIDTH + SSD_BC + g * SSD_STATE:SSD_WIDTH + SSD_BC + (g + 1) * SSD_STATE]
        c_gb = c_g.astype(BF16)
        cb = lax.dot_general(c_gb, b_g.astype(BF16), (((1,), (1,)), ((), ())),
                             preferred_element_type=F32)
        b_gt = b_g.T
        gw = heads_per_group * SSD_HEAD_DIM
        y_off = jnp.dot(c_gb, state[:, g * gw:(g + 1) * gw].astype(BF16),
                        preferred_element_type=F32)
        for pr in range(heads_per_group // 2):
            h0 = g * heads_per_group + 2 * pr
            ps = slice(h0 * SSD_HEAD_DIM, (h0 + 2) * SSD_HEAD_DIM)
            x_pair = xbc[:, ps]
            x_lo = jnp.where(low_half, x_pair, 0.0).astype(BF16)
            x_hi = jnp.where(low_half, 0.0, x_pair).astype(BF16)
            y_pair = None
            s_pair = None
            ecols = []
            for hh, x_m in ((h0, x_lo), (h0 + 1, x_hi)):
                colv = jnp.broadcast_to(a_cum[:, hh:hh + 1], (t, t))
                rowv = a_cum_t[hh:hh + 1, :]
                decay = jnp.where(tril, jnp.exp(colv - rowv), 0.0)
                m = (cb * decay * dt_t[hh:hh + 1, :]).astype(BF16)
                yd = jnp.dot(m, x_m, preferred_element_type=F32)
                y_pair = yd if y_pair is None else y_pair + yd
                wm = (b_gt * w_t[hh:hh + 1, :]).astype(BF16)
                sn = jnp.dot(wm, x_m, preferred_element_type=F32)
                s_pair = sn if s_pair is None else s_pair + sn
                ecols.append(jnp.exp(colv))
            e_pair = jnp.where(low_half, ecols[0], ecols[1])
            off = y_off[:, 2 * pr * SSD_HEAD_DIM:(2 * pr + 2) * SSD_HEAD_DIM]
            ybuf[:, ps] = y_pair + e_pair * off + dsk_ref[:, ps] * x_pair
            cd = jnp.where(low_half[0:1, :], chunk_decay[:, h0:h0 + 1], chunk_decay[:, h0 + 1:h0 + 2])
            state[:, ps] = state[:, ps] * cd + s_pair

    gwid = SSD_WIDTH // SSD_GROUPS
    rows = 32
    for g in range(SSD_GROUPS):
        gs = slice(g * gwid, (g + 1) * gwid)
        for r in range(t // rows):
            rs = slice(r * rows, (r + 1) * rows)
            zz = p_ref[rs, gs].astype(F32)
            yg = ybuf[rs, gs] * (zz * _sigmoid(zz))
            ms = jnp.mean(yg * yg, axis=-1, keepdims=True)
            o_ref[rs, gs] = (yg * lax.rsqrt(ms + EPS) * nw_ref[:, gs]).astype(o_ref.dtype)


def ssd_group(p_ssd, h, layer, w_t, conv_w, conv_b, dtb, alog, dsk, norm_w):
    seq = p_ssd.shape[0]
    t = SSD_T
    assert SSD_COLS % SSD_HEADS == 0
    per_layer = lambda *shape: pl.BlockSpec((None,) + shape, lambda i: (layer,) + (0,) * len(shape))
    return pl.pallas_call(
        _ssd_kernel,
        grid=(seq // t,),
        in_specs=[pl.BlockSpec((t, SSD_COLS), lambda i: (i, 0)),
                  pl.BlockSpec((t, D_MODEL), lambda i: (i, 0)),
                  pl.BlockSpec((None, SSD_HEADS, D_MODEL), lambda i: (layer, SSD_COLS // SSD_HEADS, 0)),
                  per_layer(SSD_CONV, CONV_DIM),
                  per_layer(1, CONV_DIM),
                  per_layer(1, LANES),
                  per_layer(1, LANES),
                  per_layer(1, SSD_WIDTH),
                  per_layer(1, SSD_WIDTH)],
        out_specs=pl.BlockSpec((t, SSD_WIDTH), lambda i: (i, 0)),
        out_shape=jax.ShapeDtypeStruct((seq, SSD_WIDTH), BF16),
        scratch_shapes=[pltpu.VMEM((t + SUBLANES, CONV_DIM), F32),
                        pltpu.VMEM((t, CONV_DIM), F32),
                        pltpu.VMEM((SSD_STATE, SSD_WIDTH), F32),
                        pltpu.VMEM((t, SSD_WIDTH), F32),
                        pltpu.VMEM((LANES, D_MODEL), BF16)],
        compiler_params=_cparams(("arbitrary",)),
        name="ssd_group",
    )(p_ssd, h, w_t, conv_w, conv_b, dtb, alog, dsk, norm_w)


def _sba_kernel(q_ref, k_ref, v_ref, g_ref, o_ref, c_ref, acc_ref):
    rows = SBA_ROWS
    kw = SBA_KEYS
    qt = q_ref.shape[1]
    n_sub = qt // rows
    subs = range(n_sub)
    qi = pl.program_id(1)
    scale = 1.0 / math.sqrt(SBA_DIM)
    row = lax.broadcasted_iota(jnp.int32, (rows, kw), 0)
    col = lax.broadcasted_iota(jnp.int32, (rows, kw), 1)
    col_minus_row = col - row
    r2 = lax.broadcasted_iota(jnp.int32, (2 * kw, 2 * kw), 0)
    c2 = lax.broadcasted_iota(jnp.int32, (2 * kw, 2 * kw), 1)
    r2 = jnp.where(r2 >= kw, r2 - kw, r2)
    neg_cum = jnp.where((c2 >= kw) | (r2 > c2), -1.0, 0.0).astype(BF16)

    def sweep_step(d, diag):
        q0s = [qi * qt + s * rows for s in subs]
        starts = [q0 + rows - (d + 1) * kw for q0 in q0s]
        loads = [jnp.maximum(ws, 0) for ws in starts]
        kss = [pl.ds(pl.multiple_of(ld, rows), kw) for ld in loads]
        zs = [lax.dot_general(q_ref[0, s * rows:(s + 1) * rows, :], k_ref[0, kss[s], :], NT_DIMS,
                              preferred_element_type=F32) * scale for s in subs]
        if diag:
            masks = [col_minus_row < (q0s[s] - loads[s]) for s in subs]
        else:
            masks = [col < (starts[s] + kw - loads[s]) for s in subs]
        sps = [jnp.where(masks[s], _softplus(zs[s]), 0.0) for s in subs]
        sums = []
        for g0 in range(0, n_sub, SBA_CUM_GROUP):
            parts = []
            for s in range(g0, g0 + SBA_CUM_GROUP):
                hi, lo = _split_bf16(sps[s], 2)
                parts.append(jnp.concatenate([hi, lo], axis=1))
            grp = jnp.dot(jnp.concatenate(parts, axis=0), neg_cum, preferred_element_type=F32)
            sums.extend(grp[i * rows:(i + 1) * rows] for i in range(SBA_CUM_GROUP))
        pending = None
        ws_b = []
        for s in subs:
            later = sums[s][:, :kw]
            c_old = c_ref[s]
            c_new = c_old + sums[s][:, kw:]
            c_ref[s] = c_new
            w = jnp.exp(zs[s] - sps[s] + later + c_old)
            ws_b.append(jnp.where(masks[s], w, 0.0).astype(BF16))
            want = jnp.where(starts[s] > 0, c_new, -jnp.inf)
            pending = want if pending is None else jnp.maximum(pending, want)
        for s in subs:
            acc_ref[s] += jnp.dot(ws_b[s], v_ref[0, kss[s], :], preferred_element_type=F32)
        return jnp.max(pending)

    c_ref[...] = jnp.zeros_like(c_ref)
    acc_ref[...] = jnp.zeros_like(acc_ref)
    first = sweep_step(0, True)
    lax.while_loop(lambda st: st[1] > SBA_EXIT,
                   lambda st: (st[0] + 1, sweep_step(st[0], False)),
                   (jnp.int32(1), first))
    for s in subs:
        qs = slice(s * rows, (s + 1) * rows)
        gg = g_ref[0, qs, :].astype(F32)
        o_ref[qs, :] = (acc_ref[s] * (gg * _sigmoid(gg))).astype(o_ref.dtype)


def sba_group(p_sba):
    seq = p_sba.shape[1]
    qt = min(SBA_QT, seq)
    nh = SBA_HEADS
    return pl.pallas_call(
        _sba_kernel,
        grid=(nh, seq // qt),
        in_specs=[pl.BlockSpec((1, qt, SBA_DIM), lambda h, i: (h, i, 0)),
                  pl.BlockSpec((1, seq, SBA_DIM), lambda h, i: (nh + h, 0, 0)),
                  pl.BlockSpec((1, seq, SBA_DIM), lambda h, i: (2 * nh + h, 0, 0)),
                  pl.BlockSpec((1, qt, SBA_DIM), lambda h, i: (3 * nh + h, i, 0))],
        out_specs=pl.BlockSpec((qt, SBA_DIM), lambda h, i: (i, h)),
        out_shape=jax.ShapeDtypeStruct((seq, SBA_WIDTH), BF16),
        scratch_shapes=[pltpu.VMEM((qt // SBA_ROWS, SBA_ROWS, SBA_KEYS), F32),
                        pltpu.VMEM((qt // SBA_ROWS, SBA_ROWS, SBA_DIM), F32)],
        compiler_params=_cparams(("arbitrary", "arbitrary")),
        name="sba_group",
    )(p_sba, p_sba, p_sba, p_sba)


def _out_proj_kernel(ya_ref, yb_ref, wa_ref, wb_ref, x_ref, nw_ref, *out_refs, emit_x):
    acc = jnp.dot(ya_ref[...], wa_ref[...], preferred_element_type=F32)
    acc = acc + jnp.dot(yb_ref[...], wb_ref[...], preferred_element_type=F32)
    x_new = x_ref[...] + acc
    if emit_x:
        xo_ref, h_ref = out_refs
        xo_ref[...] = x_new
    else:
        (h_ref,) = out_refs
    h_ref[...] = _rmsnorm_rows(x_new, nw_ref[...]).astype(h_ref.dtype)


def out_proj(y_ssd, y_sba, w_all, layer, x, norm_all, norm_idx, last, tm=256):
    seq, d = x.shape
    ka, kb = y_ssd.shape[1], y_sba.shape[1]
    assert ka == kb and w_all.shape[1] == ka + kb
    row_spec = lambda width: pl.BlockSpec((tm, width), lambda i: (i, 0))
    if last:
        out_specs = row_spec(d)
        out_shape = jax.ShapeDtypeStruct((seq, d), F32)
    else:
        out_specs = [row_spec(d), row_spec(d)]
        out_shape = [jax.ShapeDtypeStruct((seq, d), F32), jax.ShapeDtypeStruct((seq, d), BF16)]
    return pl.pallas_call(
        functools.partial(_out_proj_kernel, emit_x=not last),
        grid=(seq // tm,),
        in_specs=[row_spec(ka), row_spec(kb),
                  pl.BlockSpec((None, ka, d), lambda i: (layer, 0, 0)),
                  pl.BlockSpec((None, kb, d), lambda i: (layer, 1, 0)),
                  row_spec(d),
                  pl.BlockSpec((None, 1, d), lambda i: (norm_idx, 0, 0))],
        out_specs=out_specs,
        out_shape=out_shape,
        compiler_params=_cparams(("arbitrary",)),
        name="out_proj",
    )(y_ssd, y_sba, w_all, w_all, x, norm_all)


def kernel(x, norm_w, w_in, conv_w, conv_b, dt_bias, a_log, d_skip, ssd_norm_w, w_out, final_norm_w):
    bsz, seq, d = x.shape
    assert bsz == 1 and d == D_MODEL
    depth = w_in.shape[0]
    sba_col0 = SSD_COLS + SSD_HEADS
    pad = LANES - SSD_HEADS
    w_in_t = jnp.swapaxes(w_in, 1, 2)
    w_out_b = w_out.astype(BF16)
    dtb = jnp.pad(dt_bias, ((0, 0), (0, pad))).reshape(depth, 1, LANES)
    alog = jnp.pad(a_log, ((0, 0), (0, pad))).reshape(depth, 1, LANES)
    dsk = jnp.repeat(d_skip, SSD_HEAD_DIM, axis=1).reshape(depth, 1, SSD_WIDTH)
    conv_b3 = conv_b.reshape(depth, 1, CONV_DIM)
    ssd_nw = ssd_norm_w.reshape(depth, 1, SSD_WIDTH)
    norms = jnp.concatenate([norm_w, final_norm_w[None]], axis=0).reshape(depth + 1, 1, d)

    xs = x.reshape(seq, d)
    h = rmsnorm_bf16(xs, norm_w[0])
    for i in range(depth):
        p_ssd = in_proj(h, w_in_t, i, 0, SSD_COLS, False, "in_proj_ssd")
        p_sba = in_proj(h, w_in_t, i, sba_col0, 4 * SBA_WIDTH, True, "in_proj_sba")
        y_ssd = ssd_group(p_ssd, h, i, w_in_t, conv_w, conv_b3, dtb, alog, dsk, ssd_nw)
        y_sba = sba_group(p_sba)
        last = i == depth - 1
        res = out_proj(y_ssd, y_sba, w_out_b, i, xs, norms, i + 1, last)
        if last:
            out = res
        else:
            xs, h = res
    return out.reshape(bsz, seq, d)
```

```python
import functools
import math

import jax
import jax.numpy as jnp
from jax import lax
from jax.experimental import pallas as pl
from jax.experimental.pallas import tpu as pltpu

F32 = jnp.float32
BF16 = jnp.bfloat16

D_MODEL = 2048
SSD_WIDTH = 2048
SSD_HEAD_DIM = 64
SSD_HEADS = SSD_WIDTH // SSD_HEAD_DIM
SSD_GROUPS = 4
SSD_STATE = 128
SSD_CONV = 4
SSD_BC = SSD_GROUPS * SSD_STATE
CONV_DIM = SSD_WIDTH + 2 * SSD_BC
SSD_COLS = SSD_WIDTH + CONV_DIM
SBA_HEADS = 16
SBA_DIM = 128
SBA_WIDTH = SBA_HEADS * SBA_DIM
EPS = 1e-6

LANES = 128
SUBLANES = 8
VMEM_LIMIT = 56 * 1024 * 1024

SSD_T = 128
CONV_BLK = 256
SBA_ROWS = 64
SBA_KEYS = 128
SBA_CUM_GROUP = 2
SBA_SUM_PARTS = 1
SBA_QT = 2048
SBA_EXIT = -104.0


def _cparams(sem):
    return pltpu.CompilerParams(dimension_semantics=sem, vmem_limit_bytes=VMEM_LIMIT)


def _sigmoid(a):
    return 1.0 / (1.0 + jnp.exp(-a))


def _silu(a):
    return a * _sigmoid(a)


def _softplus(a):
    return jnp.maximum(a, 0.0) + jnp.log(1.0 + jnp.exp(-jnp.abs(a)))


def _split_bf16(a, parts):
    out = []
    r = a
    for i in range(parts):
        p = r.astype(BF16)
        out.append(p)
        if i + 1 < parts:
            r = r - p.astype(F32)
    return out


def _rmsnorm_rows(x, w):
    ms = jnp.mean(x * x, axis=-1, keepdims=True)
    return x * lax.rsqrt(ms + EPS) * w


def _rmsnorm_kernel(x_ref, w_ref, o_ref, *, rows):
    w = w_ref[...]

    def body(r, carry):
        sl = pl.ds(pl.multiple_of(r * rows, rows), rows)
        o_ref[sl, :] = _rmsnorm_rows(x_ref[sl, :], w).astype(o_ref.dtype)
        return carry

    lax.fori_loop(0, x_ref.shape[0] // rows, body, 0)


def rmsnorm_bf16(x, w, tm=256):
    seq, d = x.shape
    return pl.pallas_call(
        functools.partial(_rmsnorm_kernel, rows=32),
        grid=(seq // tm,),
        in_specs=[pl.BlockSpec((tm, d), lambda i: (i, 0)),
                  pl.BlockSpec((1, d), lambda i: (0, 0))],
        out_specs=pl.BlockSpec((tm, d), lambda i: (i, 0)),
        out_shape=jax.ShapeDtypeStruct((seq, d), BF16),
        compiler_params=_cparams(("arbitrary",)),
        name="rmsnorm",
    )(x, w.reshape(1, d))


NT_DIMS = (((1,), (1,)), ((), ()))
W_ROW_ALIGN = 32


def _proj_kernel(h_ref, *refs, shift, heads):
    if shift:
        w_lo_ref, w_hi_ref, o_ref, wb_ref = refs
    else:
        w_lo_ref, o_ref, wb_ref = refs
    tn = wb_ref.shape[0]

    @pl.when(pl.program_id(1) == 0)
    def _():
        rows = W_ROW_ALIGN

        def body(r, carry):
            dst = pl.ds(pl.multiple_of(r * rows, rows), rows)
            src = pl.ds(pl.multiple_of(r * rows + shift, rows), rows)
            wb_ref[dst, :] = w_lo_ref[src, :].astype(BF16)
            return carry

        lax.fori_loop(0, (tn - shift) // rows, body, 0)
        if shift:
            wb_ref[tn - shift:tn, :] = w_hi_ref[...].astype(BF16)

    acc = lax.dot_general(h_ref[...], wb_ref[...], NT_DIMS, preferred_element_type=F32)
    if heads:
        for hh in range(o_ref.shape[0]):
            o_ref[hh] = acc[:, hh * LANES:(hh + 1) * LANES].astype(o_ref.dtype)
    else:
        o_ref[...] = acc.astype(o_ref.dtype)


def in_proj(h, w_t, layer, row0, n, heads, name, tm=1024, tn=1024):
    seq, k = h.shape
    shift = row0 % tn
    assert shift % W_ROW_ALIGN == 0 and n % tn == 0 and w_t.shape[1] >= row0 + n
    lo0 = row0 // tn
    in_specs = [pl.BlockSpec((tm, k), lambda j, i: (i, 0)),
                pl.BlockSpec((None, tn, k), lambda j, i: (layer, lo0 + j, 0))]
    operands = [h, w_t]
    if shift:
        assert tn % shift == 0
        per = tn // shift
        in_specs.append(pl.BlockSpec((None, shift, k), lambda j, i: (layer, (lo0 + j + 1) * per, 0)))
        operands.append(w_t)
    if heads:
        hp = tn // LANES
        out_spec = pl.BlockSpec((hp, tm, LANES), lambda j, i: (j, i, 0))
        out_shape = jax.ShapeDtypeStruct((n // LANES, seq, LANES), BF16)
    else:
        out_spec = pl.BlockSpec((tm, tn), lambda j, i: (i, j))
        out_shape = jax.ShapeDtypeStruct((seq, n), BF16)
    return pl.pallas_call(
        functools.partial(_proj_kernel, shift=shift, heads=heads),
        grid=(n // tn, seq // tm),
        in_specs=in_specs,
        out_specs=out_spec,
        out_shape=out_shape,
        scratch_shapes=[pltpu.VMEM((tn, k), BF16)],
        compiler_params=_cparams(("arbitrary", "arbitrary")),
        name=name,
    )(*operands)


def _ssd_kernel(p_ref, h_ref, wdt_ref, cw_ref, cb_ref, dtb_ref, alog_ref, dsk_ref, nw_ref, o_ref,
                hbuf, xbc, state, ybuf, wdt_b):
    t = SSD_T
    hist = SUBLANES
    c = pl.program_id(0)

    @pl.when(c == 0)
    def _():
        hbuf[...] = jnp.zeros_like(hbuf)
        state[...] = jnp.zeros_like(state)
        wdt_b[0:SSD_HEADS, :] = wdt_ref[...].astype(BF16)
        wdt_b[SSD_HEADS:LANES, :] = jnp.zeros((LANES - SSD_HEADS, D_MODEL), BF16)

    row = lax.broadcasted_iota(jnp.int32, (t, t), 0)
    col = lax.broadcasted_iota(jnp.int32, (t, t), 1)

    shift_by = {back: jnp.where(row - col == back, 1.0, 0.0).astype(BF16) for back in range(1, SSD_CONV)}
    top = lax.broadcasted_iota(jnp.int32, (hist, CONV_BLK), 0)

    def conv_blocks(j0, j1):
        for j in range(j0, j1):
            cs = slice(j * CONV_BLK, (j + 1) * CONV_BLK)
            ub = p_ref[:, SSD_WIDTH + j * CONV_BLK:SSD_WIDTH + (j + 1) * CONV_BLK]
            u = ub.astype(F32)
            prev = hbuf[:, cs]
            acc = cb_ref[:, cs] + cw_ref[SSD_CONV - 1:SSD_CONV, cs] * u
            for back in range(1, SSD_CONV):
                sh = jnp.dot(shift_by[back], ub, preferred_element_type=F32)
                head = jnp.where(top < back, pltpu.roll(prev, back, axis=0), sh[0:hist])
                sh = jnp.concatenate([head, sh[hist:]], axis=0)
                acc = acc + cw_ref[SSD_CONV - 1 - back:SSD_CONV - back, cs] * sh
            xbc[:, cs] = _silu(acc)
            hbuf[:, cs] = u[t - hist:t]

    n_blk = CONV_DIM // CONV_BLK
    dt_raw = lax.dot_general(h_ref[...], wdt_b[...], NT_DIMS, preferred_element_type=F32)
    conv_blocks(0, n_blk // 3)
    dt = _softplus(dt_raw + dtb_ref[...])
    a_head = -jnp.exp(alog_ref[...])
    da = dt * a_head
    tril = row >= col
    tril_b = jnp.where(tril, 1.0, 0.0).astype(BF16)
    a_cum = None
    for part in _split_bf16(da, 3):
        term = jnp.dot(tril_b, part, preferred_element_type=F32)
        a_cum = term if a_cum is None else a_cum + term
    conv_blocks(n_blk // 3, 2 * n_blk // 3)
    a_cum_t = a_cum.T
    dt_t = dt.T
    conv_blocks(2 * n_blk // 3, n_blk)
    a_last_t = jnp.broadcast_to(a_cum_t[:, t - 1:t], a_cum_t.shape)
    w_t = jnp.exp(a_last_t - a_cum_t) * dt_t
    chunk_decay = jnp.exp(a_cum[t - 1:t, :])
    lane = lax.broadcasted_iota(jnp.int32, (t, LANES), 1)
    low_half = lane < SSD_HEAD_DIM

    heads_per_group = SSD_HEADS // SSD_GROUPS
    for g in range(SSD_GROUPS):
        b_g = xbc[:, SSD_WIDTH + g * SSD_STATE:SSD_WIDTH + (g + 1) * SSD_STATE]
        c_g = xbc[:, SSD_WIDTH + SSD_BC + g * SSD_STATE:SSD_WIDTH + SSD_BC + (g + 1) * SSD_STATE]
        c_gb = c_g.astype(BF16)
        cb = lax.dot_general(c_gb, b_g.astype(BF16), NT_DIMS, preferred_element_type=F32)
        b_gt = b_g.T
        for pr in range(heads_per_group // 2):
            h0 = g * heads_per_group + 2 * pr
            ps = slice(h0 * SSD_HEAD_DIM, (h0 + 2) * SSD_HEAD_DIM)
            x_pair = xbc[:, ps]
            x_lo = jnp.where(low_half, x_pair, 0.0).astype(BF16)
            x_hi = jnp.where(low_half, 0.0, x_pair).astype(BF16)
            y_pair = None
            s_pair = None
            ecols = []
            for hh, x_m in ((h0, x_lo), (h0 + 1, x_hi)):
                colv = jnp.broadcast_to(a_cum[:, hh:hh + 1], (t, t))
                rowv = a_cum_t[hh:hh + 1, :]
                decay = jnp.where(tril, jnp.exp(colv - rowv), 0.0)
                m = (cb * decay * dt_t[hh:hh + 1, :]).astype(BF16)
                yd = jnp.dot(m, x_m, preferred_element_type=F32)
                y_pair = yd if y_pair is None else y_pair + yd
                wm = (b_gt * w_t[hh:hh + 1, :]).astype(BF16)
                sn = jnp.dot(wm, x_m, preferred_element_type=F32)
                s_pair = sn if s_pair is None else s_pair + sn
                ecols.append(jnp.exp(colv))
            e_pair = jnp.where(low_half, ecols[0], ecols[1])
            off = jnp.dot(c_gb, state[:, ps].astype(BF16), preferred_element_type=F32)
            ybuf[:, ps] = y_pair + e_pair * off + dsk_ref[:, ps] * x_pair
            cd = jnp.where(low_half[0:1, :], chunk_decay[:, h0:h0 + 1], chunk_decay[:, h0 + 1:h0 + 2])
            state[:, ps] = state[:, ps] * cd + s_pair

    gwid = SSD_WIDTH // SSD_GROUPS
    rows = 32
    for g in range(SSD_GROUPS):
        gs = slice(g * gwid, (g + 1) * gwid)
        for r in range(t // rows):
            rs = slice(r * rows, (r + 1) * rows)
            yg = ybuf[rs, gs] * _silu(p_ref[rs, gs].astype(F32))
            ms = jnp.mean(yg * yg, axis=-1, keepdims=True)
            o_ref[rs, gs] = (yg * lax.rsqrt(ms + EPS) * nw_ref[:, gs]).astype(o_ref.dtype)


def ssd_group(p_ssd, h, layer, w_t, conv_w, conv_b, dtb, alog, dsk, norm_w):
    seq = p_ssd.shape[0]
    t = SSD_T
    assert SSD_COLS % SSD_HEADS == 0
    per_layer = lambda *shape: pl.BlockSpec((None,) + shape, lambda i: (layer,) + (0,) * len(shape))
    return pl.pallas_call(
        _ssd_kernel,
        grid=(seq // t,),
        in_specs=[pl.BlockSpec((t, SSD_COLS), lambda i: (i, 0)),
                  pl.BlockSpec((t, D_MODEL), lambda i: (i, 0)),
                  pl.BlockSpec((None, SSD_HEADS, D_MODEL), lambda i: (layer, SSD_COLS // SSD_HEADS, 0)),
                  per_layer(SSD_CONV, CONV_DIM),
                  per_layer(1, CONV_DIM),
                  per_layer(1, LANES),
                  per_layer(1, LANES),
                  per_layer(1, SSD_WIDTH),
                  per_layer(1, SSD_WIDTH)],
        out_specs=pl.BlockSpec((t, SSD_WIDTH), lambda i: (i, 0)),
        out_shape=jax.ShapeDtypeStruct((seq, SSD_WIDTH), BF16),
        scratch_shapes=[pltpu.VMEM((SUBLANES, CONV_DIM), F32),
                        pltpu.VMEM((t, CONV_DIM), F32),
                        pltpu.VMEM((SSD_STATE, SSD_WIDTH), F32),
                        pltpu.VMEM((t, SSD_WIDTH), F32),
                        pltpu.VMEM((LANES, D_MODEL), BF16)],
        compiler_params=_cparams(("arbitrary",)),
        name="ssd_group",
    )(p_ssd, h, w_t, conv_w, conv_b, dtb, alog, dsk, norm_w)


def _sba_kernel(q_ref, k_ref, v_ref, g_ref, o_ref, c_ref, acc_ref):
    rows = SBA_ROWS
    kw = SBA_KEYS
    qt = q_ref.shape[1]
    n_sub = qt // rows
    subs = range(n_sub)
    qi = pl.program_id(1)
    scale = 1.0 / math.sqrt(SBA_DIM)
    row = lax.broadcasted_iota(jnp.int32, (rows, kw), 0)
    col = lax.broadcasted_iota(jnp.int32, (rows, kw), 1)
    col_minus_row = col - row
    r2 = lax.broadcasted_iota(jnp.int32, (SBA_SUM_PARTS * kw, 2 * kw), 0) % kw
    c2 = lax.broadcasted_iota(jnp.int32, (SBA_SUM_PARTS * kw, 2 * kw), 1)
    neg_cum = jnp.where((c2 >= kw) | (r2 > c2), -1.0, 0.0).astype(BF16)

    def sweep_step(d, diag):
        q0s = [qi * qt + s * rows for s in subs]
        starts = [q0 + rows - (d + 1) * kw for q0 in q0s]
        loads = [jnp.maximum(ws, 0) for ws in starts]
        kss = [pl.ds(pl.multiple_of(ld, rows), kw) for ld in loads]
        zs = [lax.dot_general(q_ref[0, s * rows:(s + 1) * rows, :], k_ref[0, kss[s], :], NT_DIMS,
                              preferred_element_type=F32) * scale for s in subs]
        if diag:
            masks = [col_minus_row < (q0s[s] - loads[s]) for s in subs]
        else:
            masks = [col < (starts[s] + kw - loads[s]) for s in subs]
        sps = [jnp.where(masks[s], _softplus(zs[s]), 0.0) for s in subs]
        sums = []
        for g0 in range(0, n_sub, SBA_CUM_GROUP):
            parts = []
            for s in range(g0, g0 + SBA_CUM_GROUP):
                parts.append(jnp.concatenate(_split_bf16(sps[s], SBA_SUM_PARTS), axis=1))
            grp = jnp.dot(jnp.concatenate(parts, axis=0), neg_cum, preferred_element_type=F32)
            sums.extend(grp[i * rows:(i + 1) * rows] for i in range(SBA_CUM_GROUP))
        pending = None
        ws_b = []
        for s in subs:
            later = sums[s][:, :kw]
            c_old = c_ref[s]
            c_new = c_old + sums[s][:, kw:]
            c_ref[s] = c_new
            w = jnp.exp(zs[s] - sps[s] + later + c_old)
            ws_b.append(jnp.where(masks[s], w, 0.0).astype(BF16))
            want = jnp.where(starts[s] > 0, c_new, -jnp.inf)
            pending = want if pending is None else jnp.maximum(pending, want)
        for s in subs:
            acc_ref[s] += jnp.dot(ws_b[s], v_ref[0, kss[s], :], preferred_element_type=F32)
        return jnp.max(pending)

    c_ref[...] = jnp.zeros_like(c_ref)
    acc_ref[...] = jnp.zeros_like(acc_ref)
    first = sweep_step(0, True)
    lax.while_loop(lambda st: st[1] > SBA_EXIT,
                   lambda st: (st[0] + 1, sweep_step(st[0], False)),
                   (jnp.int32(1), first))
    for s in subs:
        qs = slice(s * rows, (s + 1) * rows)
        o_ref[qs, :] = (acc_ref[s] * _silu(g_ref[0, qs, :].astype(F32))).astype(o_ref.dtype)


def sba_group(p_sba):
    seq = p_sba.shape[1]
    qt = min(SBA_QT, seq)
    nh = SBA_HEADS
    return pl.pallas_call(
        _sba_kernel,
        grid=(nh, seq // qt),
        in_specs=[pl.BlockSpec((1, qt, SBA_DIM), lambda h, i: (h, i, 0)),
                  pl.BlockSpec((1, seq, SBA_DIM), lambda h, i: (nh + h, 0, 0)),
                  pl.BlockSpec((1, seq, SBA_DIM), lambda h, i: (2 * nh + h, 0, 0)),
                  pl.BlockSpec((1, qt, SBA_DIM), lambda h, i: (3 * nh + h, i, 0))],
        out_specs=pl.BlockSpec((qt, SBA_DIM), lambda h, i: (i, h)),
        out_shape=jax.ShapeDtypeStruct((seq, SBA_WIDTH), BF16),
        scratch_shapes=[pltpu.VMEM((qt // SBA_ROWS, SBA_ROWS, SBA_KEYS), F32),
                        pltpu.VMEM((qt // SBA_ROWS, SBA_ROWS, SBA_DIM), F32)],
        compiler_params=_cparams(("arbitrary", "arbitrary")),
        name="sba_group",
    )(p_sba, p_sba, p_sba, p_sba)


def _out_proj_kernel(ya_ref, yb_ref, wa_ref, wb_ref, x_ref, nw_ref, *out_refs, emit_x):
    acc = jnp.dot(ya_ref[...], wa_ref[...], preferred_element_type=F32)
    acc = acc + jnp.dot(yb_ref[...], wb_ref[...], preferred_element_type=F32)
    x_new = x_ref[...] + acc
    if emit_x:
        xo_ref, h_ref = out_refs
        xo_ref[...] = x_new
    else:
        (h_ref,) = out_refs
    h_ref[...] = _rmsnorm_rows(x_new, nw_ref[...]).astype(h_ref.dtype)


def out_proj(y_ssd, y_sba, w_all, layer, x, norm_all, norm_idx, last, tm=256):
    seq, d = x.shape
    ka, kb = y_ssd.shape[1], y_sba.shape[1]
    assert ka == kb and w_all.shape[1] == ka + kb
    row_spec = lambda width: pl.BlockSpec((tm, width), lambda i: (i, 0))
    if last:
        out_specs = row_spec(d)
        out_shape = jax.ShapeDtypeStruct((seq, d), F32)
    else:
        out_specs = [row_spec(d), row_spec(d)]
        out_shape = [jax.ShapeDtypeStruct((seq, d), F32), jax.ShapeDtypeStruct((seq, d), BF16)]
    return pl.pallas_call(
        functools.partial(_out_proj_kernel, emit_x=not last),
        grid=(seq // tm,),
        in_specs=[row_spec(ka), row_spec(kb),
                  pl.BlockSpec((None, ka, d), lambda i: (layer, 0, 0)),
                  pl.BlockSpec((None, kb, d), lambda i: (layer, 1, 0)),
                  row_spec(d),
                  pl.BlockSpec((None, 1, d), lambda i: (norm_idx, 0, 0))],
        out_specs=out_specs,
        out_shape=out_shape,
        compiler_params=_cparams(("arbitrary",)),
        name="out_proj",
    )(y_ssd, y_sba, w_all, w_all, x, norm_all)


def kernel(x, norm_w, w_in, conv_w, conv_b, dt_bias, a_log, d_skip, ssd_norm_w, w_out, final_norm_w):
    bsz, seq, d = x.shape
    assert bsz == 1 and d == D_MODEL
    depth = w_in.shape[0]
    sba_row0 = SSD_COLS + SSD_HEADS
    pad = LANES - SSD_HEADS
    w_in_t = jnp.swapaxes(w_in, 1, 2)
    w_out_b = w_out.astype(BF16)
    dtb = jnp.pad(dt_bias, ((0, 0), (0, pad))).reshape(depth, 1, LANES)
    alog = jnp.pad(a_log, ((0, 0), (0, pad))).reshape(depth, 1, LANES)
    dsk = jnp.repeat(d_skip, SSD_HEAD_DIM, axis=1).reshape(depth, 1, SSD_WIDTH)
    conv_b3 = conv_b.reshape(depth, 1, CONV_DIM)
    ssd_nw = ssd_norm_w.reshape(depth, 1, SSD_WIDTH)
    norms = jnp.concatenate([norm_w, final_norm_w[None]], axis=0).reshape(depth + 1, 1, d)

    xs = x.reshape(seq, d)
    h = rmsnorm_bf16(xs, norm_w[0])
    for i in range(depth):
        p_ssd = in_proj(h, w_in_t, i, 0, SSD_COLS, False, "in_proj_ssd")
        p_sba = in_proj(h, w_in_t, i, sba_row0, 4 * SBA_WIDTH, True, "in_proj_sba")
        y_ssd = ssd_group(p_ssd, h, i, w_in_t, conv_w, conv_b3, dtb, alog, dsk, ssd_nw)
        y_sba = sba_group(p_sba)
        last = i == depth - 1
        res = out_proj(y_ssd, y_sba, w_out_b, i, xs, norms, i + 1, last)
        if last:
            out = res
        else:
            xs, h = res
    return out.reshape(bsz, seq, d)
```

```python
import functools
import math

import jax
import jax.numpy as jnp
from jax import lax
from jax.experimental import pallas as pl
from jax.experimental.pallas import tpu as pltpu

F32 = jnp.float32
BF16 = jnp.bfloat16

D_MODEL = 2048
SSD_WIDTH = 2048
SSD_HEAD_DIM = 64
SSD_HEADS = SSD_WIDTH // SSD_HEAD_DIM
SSD_GROUPS = 4
SSD_STATE = 128
SSD_CONV = 4
SSD_BC = SSD_GROUPS * SSD_STATE
CONV_DIM = SSD_WIDTH + 2 * SSD_BC
SSD_COLS = SSD_WIDTH + CONV_DIM
SBA_HEADS = 16
SBA_DIM = 128
SBA_WIDTH = SBA_HEADS * SBA_DIM
EPS = 1e-6

LANES = 128
SUBLANES = 8
VMEM_LIMIT = 56 * 1024 * 1024

SSD_T = 128
CONV_BLK = 256
SBA_ROWS = 64
SBA_KEYS = 128
SBA_CUM_GROUP = 2
SBA_SUM_PARTS = 1
SBA_OPENING_STEPS = 2
SBA_STAGE_SKEW = 6
SBA_QT = 2048
SBA_EXIT = -104.0


def _cparams(sem):
    return pltpu.CompilerParams(dimension_semantics=sem, vmem_limit_bytes=VMEM_LIMIT)


def _sigmoid(a):
    return 1.0 / (1.0 + jnp.exp(-a))


def _silu(a):
    return a * _sigmoid(a)


def _softplus(a):
    return jnp.maximum(a, 0.0) + jnp.log(1.0 + jnp.exp(-jnp.abs(a)))


def _split_bf16(a, parts):
    out = []
    r = a
    for i in range(parts):
        p = r.astype(BF16)
        out.append(p)
        if i + 1 < parts:
            r = r - p.astype(F32)
    return out


def _rmsnorm_rows(x, w):
    ms = jnp.mean(x * x, axis=-1, keepdims=True)
    return x * lax.rsqrt(ms + EPS) * w


def _rmsnorm_kernel(x_ref, w_ref, o_ref, *, rows):
    w = w_ref[...]

    def body(r, carry):
        sl = pl.ds(pl.multiple_of(r * rows, rows), rows)
        o_ref[sl, :] = _rmsnorm_rows(x_ref[sl, :], w).astype(o_ref.dtype)
        return carry

    lax.fori_loop(0, x_ref.shape[0] // rows, body, 0)


def rmsnorm_bf16(x, w, tm=256):
    seq, d = x.shape
    return pl.pallas_call(
        functools.partial(_rmsnorm_kernel, rows=32),
        grid=(seq // tm,),
        in_specs=[pl.BlockSpec((tm, d), lambda i: (i, 0)),
                  pl.BlockSpec((1, d), lambda i: (0, 0))],
        out_specs=pl.BlockSpec((tm, d), lambda i: (i, 0)),
        out_shape=jax.ShapeDtypeStruct((seq, d), BF16),
        compiler_params=_cparams(("arbitrary",)),
        name="rmsnorm",
    )(x, w.reshape(1, d))


NT_DIMS = (((1,), (1,)), ((), ()))
W_ROW_ALIGN = 32


def _proj_kernel(h_ref, *refs, shift, heads):
    if shift:
        w_lo_ref, w_hi_ref, o_ref, wb_ref = refs
    else:
        w_lo_ref, o_ref, wb_ref = refs
    tn = wb_ref.shape[0]

    @pl.when(pl.program_id(1) == 0)
    def _():
        rows = W_ROW_ALIGN

        def body(r, carry):
            dst = pl.ds(pl.multiple_of(r * rows, rows), rows)
            src = pl.ds(pl.multiple_of(r * rows + shift, rows), rows)
            wb_ref[dst, :] = w_lo_ref[src, :].astype(BF16)
            return carry

        lax.fori_loop(0, (tn - shift) // rows, body, 0)
        if shift:
            wb_ref[tn - shift:tn, :] = w_hi_ref[...].astype(BF16)

    acc = lax.dot_general(h_ref[...], wb_ref[...], NT_DIMS, preferred_element_type=F32)
    if heads:
        for hh in range(o_ref.shape[0]):
            o_ref[hh] = acc[:, hh * LANES:(hh + 1) * LANES].astype(o_ref.dtype)
    else:
        o_ref[...] = acc.astype(o_ref.dtype)


def in_proj(h, w_t, layer, row0, n, heads, name, tm=1024, tn=1024):
    seq, k = h.shape
    shift = row0 % tn
    assert shift % W_ROW_ALIGN == 0 and n % tn == 0 and w_t.shape[1] >= row0 + n
    lo0 = row0 // tn
    in_specs = [pl.BlockSpec((tm, k), lambda j, i: (i, 0)),
                pl.BlockSpec((None, tn, k), lambda j, i: (layer, lo0 + j, 0))]
    operands = [h, w_t]
    if shift:
        assert tn % shift == 0
        per = tn // shift
        in_specs.append(pl.BlockSpec((None, shift, k), lambda j, i: (layer, (lo0 + j + 1) * per, 0)))
        operands.append(w_t)
    if heads:
        hp = tn // LANES
        out_spec = pl.BlockSpec((hp, tm, LANES), lambda j, i: (j, i, 0))
        out_shape = jax.ShapeDtypeStruct((n // LANES, seq, LANES), BF16)
    else:
        out_spec = pl.BlockSpec((tm, tn), lambda j, i: (i, j))
        out_shape = jax.ShapeDtypeStruct((seq, n), BF16)
    return pl.pallas_call(
        functools.partial(_proj_kernel, shift=shift, heads=heads),
        grid=(n // tn, seq // tm),
        in_specs=in_specs,
        out_specs=out_spec,
        out_shape=out_shape,
        scratch_shapes=[pltpu.VMEM((tn, k), BF16)],
        compiler_params=_cparams(("arbitrary", "arbitrary")),
        name=name,
    )(*operands)


def _ssd_kernel(p_ref, h_ref, wdt_ref, cw_ref, cb_ref, dtb_ref, alog_ref, dsk_ref, nw_ref, o_ref,
                hbuf, xbc, state, ybuf, wdt_b):
    t = SSD_T
    hist = SUBLANES
    c = pl.program_id(0)

    @pl.when(c == 0)
    def _():
        hbuf[...] = jnp.zeros_like(hbuf)
        state[...] = jnp.zeros_like(state)
        wdt_b[0:SSD_HEADS, :] = wdt_ref[...].astype(BF16)
        wdt_b[SSD_HEADS:LANES, :] = jnp.zeros((LANES - SSD_HEADS, D_MODEL), BF16)

    row = lax.broadcasted_iota(jnp.int32, (t, t), 0)
    col = lax.broadcasted_iota(jnp.int32, (t, t), 1)

    shift_by = {back: jnp.where(row - col == back, 1.0, 0.0).astype(BF16) for back in range(1, SSD_CONV)}
    top = lax.broadcasted_iota(jnp.int32, (hist, CONV_BLK), 0)

    def conv_blocks(j0, j1):
        for j in range(j0, j1):
            cs = slice(j * CONV_BLK, (j + 1) * CONV_BLK)
            ub = p_ref[:, SSD_WIDTH + j * CONV_BLK:SSD_WIDTH + (j + 1) * CONV_BLK]
            u = ub.astype(F32)
            prev = hbuf[:, cs]
            acc = cb_ref[:, cs] + cw_ref[SSD_CONV - 1:SSD_CONV, cs] * u
            for back in range(1, SSD_CONV):
                sh = jnp.dot(shift_by[back], ub, preferred_element_type=F32)
                head = jnp.where(top < back, pltpu.roll(prev, back, axis=0), sh[0:hist])
                sh = jnp.concatenate([head, sh[hist:]], axis=0)
                acc = acc + cw_ref[SSD_CONV - 1 - back:SSD_CONV - back, cs] * sh
            xbc[:, cs] = _silu(acc)
            hbuf[:, cs] = u[t - hist:t]

    n_blk = CONV_DIM // CONV_BLK
    dt_raw = lax.dot_general(h_ref[...], wdt_b[...], NT_DIMS, preferred_element_type=F32)
    conv_blocks(0, n_blk // 3)
    dt = _softplus(dt_raw + dtb_ref[...])
    a_head = -jnp.exp(alog_ref[...])
    da = dt * a_head
    tril = row >= col
    tril_b = jnp.where(tril, 1.0, 0.0).astype(BF16)
    a_cum = None
    for part in _split_bf16(da, 3):
        term = jnp.dot(tril_b, part, preferred_element_type=F32)
        a_cum = term if a_cum is None else a_cum + term
    conv_blocks(n_blk // 3, 2 * n_blk // 3)
    a_cum_t = a_cum.T
    dt_t = dt.T
    conv_blocks(2 * n_blk // 3, n_blk)
    a_last_t = jnp.broadcast_to(a_cum_t[:, t - 1:t], a_cum_t.shape)
    w_t = jnp.exp(a_last_t - a_cum_t) * dt_t
    chunk_decay = jnp.exp(a_cum[t - 1:t, :])
    lane = lax.broadcasted_iota(jnp.int32, (t, LANES), 1)
    low_half = lane < SSD_HEAD_DIM

    heads_per_group = SSD_HEADS // SSD_GROUPS
    for g in range(SSD_GROUPS):
        b_g = xbc[:, SSD_WIDTH + g * SSD_STATE:SSD_WIDTH + (g + 1) * SSD_STATE]
        c_g = xbc[:, SSD_WIDTH + SSD_BC + g * SSD_STATE:SSD_WIDTH + SSD_BC + (g + 1) * SSD_STATE]
        c_gb = c_g.astype(BF16)
        cb = lax.dot_general(c_gb, b_g.astype(BF16), NT_DIMS, preferred_element_type=F32)
        b_gt = b_g.T
        for pr in range(heads_per_group // 2):
            h0 = g * heads_per_group + 2 * pr
            ps = slice(h0 * SSD_HEAD_DIM, (h0 + 2) * SSD_HEAD_DIM)
            x_pair = xbc[:, ps]
            x_lo = jnp.where(low_half, x_pair, 0.0).astype(BF16)
            x_hi = jnp.where(low_half, 0.0, x_pair).astype(BF16)
            y_pair = None
            s_pair = None
            ecols = []
            for hh, x_m in ((h0, x_lo), (h0 + 1, x_hi)):
                colv = jnp.broadcast_to(a_cum[:, hh:hh + 1], (t, t))
                rowv = a_cum_t[hh:hh + 1, :]
                decay = jnp.where(tril, jnp.exp(colv - rowv), 0.0)
                m = (cb * decay * dt_t[hh:hh + 1, :]).astype(BF16)
                yd = jnp.dot(m, x_m, preferred_element_type=F32)
                y_pair = yd if y_pair is None else y_pair + yd
                wm = (b_gt * w_t[hh:hh + 1, :]).astype(BF16)
                sn = jnp.dot(wm, x_m, preferred_element_type=F32)
                s_pair = sn if s_pair is None else s_pair + sn
                ecols.append(jnp.exp(colv))
            e_pair = jnp.where(low_half, ecols[0], ecols[1])
            off = jnp.dot(c_gb, state[:, ps].astype(BF16), preferred_element_type=F32)
            ybuf[:, ps] = y_pair + e_pair * off + dsk_ref[:, ps] * x_pair
            cd = jnp.where(low_half[0:1, :], chunk_decay[:, h0:h0 + 1], chunk_decay[:, h0 + 1:h0 + 2])
            state[:, ps] = state[:, ps] * cd + s_pair

    gwid = SSD_WIDTH // SSD_GROUPS
    rows = 32
    for g in range(SSD_GROUPS):
        gs = slice(g * gwid, (g + 1) * gwid)
        for r in range(t // rows):
            rs = slice(r * rows, (r + 1) * rows)
            yg = ybuf[rs, gs] * _silu(p_ref[rs, gs].astype(F32))
            ms = jnp.mean(yg * yg, axis=-1, keepdims=True)
            o_ref[rs, gs] = (yg * lax.rsqrt(ms + EPS) * nw_ref[:, gs]).astype(o_ref.dtype)


def ssd_group(p_ssd, h, layer, w_t, conv_w, conv_b, dtb, alog, dsk, norm_w):
    seq = p_ssd.shape[0]
    t = SSD_T
    assert SSD_COLS % SSD_HEADS == 0
    per_layer = lambda *shape: pl.BlockSpec((None,) + shape, lambda i: (layer,) + (0,) * len(shape))
    return pl.pallas_call(
        _ssd_kernel,
        grid=(seq // t,),
        in_specs=[pl.BlockSpec((t, SSD_COLS), lambda i: (i, 0)),
                  pl.BlockSpec((t, D_MODEL), lambda i: (i, 0)),
                  pl.BlockSpec((None, SSD_HEADS, D_MODEL), lambda i: (layer, SSD_COLS // SSD_HEADS, 0)),
                  per_layer(SSD_CONV, CONV_DIM),
                  per_layer(1, CONV_DIM),
                  per_layer(1, LANES),
                  per_layer(1, LANES),
                  per_layer(1, SSD_WIDTH),
                  per_layer(1, SSD_WIDTH)],
        out_specs=pl.BlockSpec((t, SSD_WIDTH), lambda i: (i, 0)),
        out_shape=jax.ShapeDtypeStruct((seq, SSD_WIDTH), BF16),
        scratch_shapes=[pltpu.VMEM((SUBLANES, CONV_DIM), F32),
                        pltpu.VMEM((t, CONV_DIM), F32),
                        pltpu.VMEM((SSD_STATE, SSD_WIDTH), F32),
                        pltpu.VMEM((t, SSD_WIDTH), F32),
                        pltpu.VMEM((LANES, D_MODEL), BF16)],
        compiler_params=_cparams(("arbitrary",)),
        name="ssd_group",
    )(p_ssd, h, w_t, conv_w, conv_b, dtb, alog, dsk, norm_w)


def _sba_kernel(q_ref, k_ref, v_ref, g_ref, o_ref, c_ref, acc_ref):
    rows = SBA_ROWS
    kw = SBA_KEYS
    qt = q_ref.shape[1]
    n_sub = qt // rows
    subs = range(n_sub)
    qi = pl.program_id(1)
    scale = 1.0 / math.sqrt(SBA_DIM)
    row = lax.broadcasted_iota(jnp.int32, (rows, kw), 0)
    col = lax.broadcasted_iota(jnp.int32, (rows, kw), 1)
    col_minus_row = col - row
    r2 = lax.broadcasted_iota(jnp.int32, (SBA_SUM_PARTS * kw, 2 * kw), 0) % kw
    c2 = lax.broadcasted_iota(jnp.int32, (SBA_SUM_PARTS * kw, 2 * kw), 1)
    neg_cum = jnp.where((c2 >= kw) | (r2 > c2), -1.0, 0.0).astype(BF16)

    def sweep(ds, opening):
        grp = SBA_CUM_GROUP
        units = [(i, g0) for g0 in range(0, n_sub, grp) for i in range(len(ds))]
        last_i = len(ds) - 1
        zs, masks, sps, sums, ws_b, kss, starts = {}, {}, {}, {}, {}, {}, {}
        carries = {}
        wants = []

        def stage_scores(i, g0):
            for s in range(g0, g0 + grp):
                q0 = qi * qt + s * rows
                start = q0 + rows - (ds[i] + 1) * kw
                load = jnp.maximum(start, 0)
                starts[i, s] = start
                kss[i, s] = pl.ds(pl.multiple_of(load, rows), kw)
                zs[i, s] = lax.dot_general(q_ref[0, s * rows:(s + 1) * rows, :], k_ref[0, kss[i, s], :], NT_DIMS,
                                           preferred_element_type=F32) * scale
                if opening and i == 0:
                    masks[i, s] = col_minus_row < (q0 - load)
                else:
                    masks[i, s] = col < (start + kw - load)

        def stage_sums(i, g0):
            parts = []
            for s in range(g0, g0 + grp):
                sps[i, s] = jnp.where(masks[i, s], _softplus(zs[i, s]), 0.0)
                parts.append(jnp.concatenate(_split_bf16(sps[i, s], SBA_SUM_PARTS), axis=1))
            both = jnp.dot(jnp.concatenate(parts, axis=0), neg_cum, preferred_element_type=F32)
            for j, s in enumerate(range(g0, g0 + grp)):
                sums[i, s] = both[j * rows:(j + 1) * rows]

        def stage_weights(i, g0):
            for s in range(g0, g0 + grp):
                if i == 0:
                    carries[s] = None if opening else c_ref[s]
                logw = zs[i, s] - sps[i, s] + sums[i, s][:, :kw]
                if carries[s] is not None:
                    logw = logw + carries[s]
                ws_b[i, s] = jnp.where(masks[i, s], jnp.exp(logw), 0.0).astype(BF16)
                tot = sums[i, s][:, kw:]
                carries[s] = tot if carries[s] is None else carries[s] + tot
                if i == last_i:
                    c_ref[s] = carries[s]
                    wants.append(jnp.where(starts[i, s] > 0, carries[s], -jnp.inf))

        def stage_values(i, g0):
            for s in range(g0, g0 + grp):
                pv = jnp.dot(ws_b[i, s], v_ref[0, kss[i, s], :], preferred_element_type=F32)
                if opening and i == 0:
                    acc_ref[s] = pv
                else:
                    acc_ref[s] += pv

        stages = (stage_scores, stage_sums, stage_weights, stage_values)
        skew = SBA_STAGE_SKEW
        for tick in range(len(units) + (len(stages) - 1) * skew):
            for j in reversed(range(len(stages))):
                u = tick - j * skew
                if 0 <= u < len(units):
                    stages[j](*units[u])
        pending = wants[0]
        for want in wants[1:]:
            pending = jnp.maximum(pending, want)
        return jnp.max(pending)

    n_open = SBA_OPENING_STEPS
    first = sweep(list(range(n_open)), True)
    lax.while_loop(lambda st: st[1] > SBA_EXIT,
                   lambda st: (st[0] + 1, sweep([st[0]], False)),
                   (jnp.int32(n_open), first))
    for s in subs:
        qs = slice(s * rows, (s + 1) * rows)
        o_ref[qs, :] = (acc_ref[s] * _silu(g_ref[0, qs, :].astype(F32))).astype(o_ref.dtype)


def sba_group(p_sba):
    seq = p_sba.shape[1]
    qt = min(SBA_QT, seq)
    nh = SBA_HEADS
    return pl.pallas_call(
        _sba_kernel,
        grid=(nh, seq // qt),
        in_specs=[pl.BlockSpec((1, qt, SBA_DIM), lambda h, i: (h, i, 0)),
                  pl.BlockSpec((1, seq, SBA_DIM), lambda h, i: (nh + h, 0, 0)),
                  pl.BlockSpec((1, seq, SBA_DIM), lambda h, i: (2 * nh + h, 0, 0)),
                  pl.BlockSpec((1, qt, SBA_DIM), lambda h, i: (3 * nh + h, i, 0))],
        out_specs=pl.BlockSpec((qt, SBA_DIM), lambda h, i: (i, h)),
        out_shape=jax.ShapeDtypeStruct((seq, SBA_WIDTH), BF16),
        scratch_shapes=[pltpu.VMEM((qt // SBA_ROWS, SBA_ROWS, SBA_KEYS), F32),
                        pltpu.VMEM((qt // SBA_ROWS, SBA_ROWS, SBA_DIM), F32)],
        compiler_params=_cparams(("arbitrary", "arbitrary")),
        name="sba_group",
    )(p_sba, p_sba, p_sba, p_sba)


def _out_proj_kernel(ya_ref, yb_ref, wa_ref, wb_ref, x_ref, nw_ref, *out_refs, emit_x):
    acc = jnp.dot(ya_ref[...], wa_ref[...], preferred_element_type=F32)
    acc = acc + jnp.dot(yb_ref[...], wb_ref[...], preferred_element_type=F32)
    x_new = x_ref[...] + acc
    if emit_x:
        xo_ref, h_ref = out_refs
        xo_ref[...] = x_new
    else:
        (h_ref,) = out_refs
    h_ref[...] = _rmsnorm_rows(x_new, nw_ref[...]).astype(h_ref.dtype)


def out_proj(y_ssd, y_sba, w_all, layer, x, norm_all, norm_idx, last, tm=256):
    seq, d = x.shape
    ka, kb = y_ssd.shape[1], y_sba.shape[1]
    assert ka == kb and w_all.shape[1] == ka + kb
    row_spec = lambda width: pl.BlockSpec((tm, width), lambda i: (i, 0))
    if last:
        out_specs = row_spec(d)
        out_shape = jax.ShapeDtypeStruct((seq, d), F32)
    else:
        out_specs = [row_spec(d), row_spec(d)]
        out_shape = [jax.ShapeDtypeStruct((seq, d), F32), jax.ShapeDtypeStruct((seq, d), BF16)]
    return pl.pallas_call(
        functools.partial(_out_proj_kernel, emit_x=not last),
        grid=(seq // tm,),
        in_specs=[row_spec(ka), row_spec(kb),
                  pl.BlockSpec((None, ka, d), lambda i: (layer, 0, 0)),
                  pl.BlockSpec((None, kb, d), lambda i: (layer, 1, 0)),
                  row_spec(d),
                  pl.BlockSpec((None, 1, d), lambda i: (norm_idx, 0, 0))],
        out_specs=out_specs,
        out_shape=out_shape,
        compiler_params=_cparams(("arbitrary",)),
        name="out_proj",
    )(y_ssd, y_sba, w_all, w_all, x, norm_all)


def kernel(x, norm_w, w_in, conv_w, conv_b, dt_bias, a_log, d_skip, ssd_norm_w, w_out, final_norm_w):
    bsz, seq, d = x.shape
    assert bsz == 1 and d == D_MODEL
    depth = w_in.shape[0]
    sba_row0 = SSD_COLS + SSD_HEADS
    pad = LANES - SSD_HEADS
    w_in_t = jnp.swapaxes(w_in, 1, 2)
    w_out_b = w_out.astype(BF16)
    dtb = jnp.pad(dt_bias, ((0, 0), (0, pad))).reshape(depth, 1, LANES)
    alog = jnp.pad(a_log, ((0, 0), (0, pad))).reshape(depth, 1, LANES)
    dsk = jnp.repeat(d_skip, SSD_HEAD_DIM, axis=1).reshape(depth, 1, SSD_WIDTH)
    conv_b3 = conv_b.reshape(depth, 1, CONV_DIM)
    ssd_nw = ssd_norm_w.reshape(depth, 1, SSD_WIDTH)
    norms = jnp.concatenate([norm_w, final_norm_w[None]], axis=0).reshape(depth + 1, 1, d)

    xs = x.reshape(seq, d)
    h = rmsnorm_bf16(xs, norm_w[0])
    for i in range(depth):
        p_ssd = in_proj(h, w_in_t, i, 0, SSD_COLS, False, "in_proj_ssd")
        p_sba = in_proj(h, w_in_t, i, sba_row0, 4 * SBA_WIDTH, True, "in_proj_sba")
        y_ssd = ssd_group(p_ssd, h, i, w_in_t, conv_w, conv_b3, dtb, alog, dsk, ssd_nw)
        y_sba = sba_group(p_sba)
        last = i == depth - 1
        res = out_proj(y_ssd, y_sba, w_out_b, i, xs, norms, i + 1, last)
        if last:
            out = res
        else:
            xs, h = res
    return out.reshape(bsz, seq, d)
```

```python
import functools
import math

import jax
import jax.numpy as jnp
from jax import lax
from jax.experimental import pallas as pl
from jax.experimental.pallas import tpu as pltpu

F32 = jnp.float32
BF16 = jnp.bfloat16

D_MODEL = 2048
SSD_WIDTH = 2048
SSD_HEAD_DIM = 64
SSD_HEADS = SSD_WIDTH // SSD_HEAD_DIM
SSD_GROUPS = 4
SSD_STATE = 128
SSD_CONV = 4
SSD_BC = SSD_GROUPS * SSD_STATE
CONV_DIM = SSD_WIDTH + 2 * SSD_BC
SSD_COLS = SSD_WIDTH + CONV_DIM
SBA_HEADS = 16
SBA_DIM = 128
SBA_WIDTH = SBA_HEADS * SBA_DIM
EPS = 1e-6

LANES = 128
SUBLANES = 8
VMEM_LIMIT = 56 * 1024 * 1024

SSD_T = 128
CONV_BLK = 256
SSD_PAIR_SKEW = 2
SSD_CHAIN_TICKS = (1, 5, 10, 16)
SBA_ROWS = 64
SBA_KEYS = 128
SBA_CUM_GROUP = 2
SBA_SUM_PARTS = 1
SBA_OPENING_STEPS = 2
SBA_STAGE_SKEW = 6
SBA_QT = 2048
SBA_EXIT = -104.0


def _cparams(sem):
    return pltpu.CompilerParams(dimension_semantics=sem, vmem_limit_bytes=VMEM_LIMIT)


def _sigmoid(a):
    return 1.0 / (1.0 + jnp.exp(-a))


def _silu(a):
    return a * _sigmoid(a)


def _softplus(a):
    return jnp.maximum(a, 0.0) + jnp.log(1.0 + jnp.exp(-jnp.abs(a)))


def _split_bf16(a, parts):
    out = []
    r = a
    for i in range(parts):
        p = r.astype(BF16)
        out.append(p)
        if i + 1 < parts:
            r = r - p.astype(F32)
    return out


def _rmsnorm_rows(x, w):
    ms = jnp.mean(x * x, axis=-1, keepdims=True)
    return x * lax.rsqrt(ms + EPS) * w


def _rmsnorm_kernel(x_ref, w_ref, o_ref, *, rows):
    w = w_ref[...]

    def body(r, carry):
        sl = pl.ds(pl.multiple_of(r * rows, rows), rows)
        o_ref[sl, :] = _rmsnorm_rows(x_ref[sl, :], w).astype(o_ref.dtype)
        return carry

    lax.fori_loop(0, x_ref.shape[0] // rows, body, 0)


def rmsnorm_bf16(x, w, tm=256):
    seq, d = x.shape
    return pl.pallas_call(
        functools.partial(_rmsnorm_kernel, rows=32),
        grid=(seq // tm,),
        in_specs=[pl.BlockSpec((tm, d), lambda i: (i, 0)),
                  pl.BlockSpec((1, d), lambda i: (0, 0))],
        out_specs=pl.BlockSpec((tm, d), lambda i: (i, 0)),
        out_shape=jax.ShapeDtypeStruct((seq, d), BF16),
        compiler_params=_cparams(("arbitrary",)),
        name="rmsnorm",
    )(x, w.reshape(1, d))


NT_DIMS = (((1,), (1,)), ((), ()))
W_ROW_ALIGN = 32


def _proj_kernel(h_ref, *refs, shift, heads):
    if shift:
        w_lo_ref, w_hi_ref, o_ref, wb_ref = refs
    else:
        w_lo_ref, o_ref, wb_ref = refs
    tn = wb_ref.shape[0]

    @pl.when(pl.program_id(1) == 0)
    def _():
        rows = W_ROW_ALIGN

        def body(r, carry):
            dst = pl.ds(pl.multiple_of(r * rows, rows), rows)
            src = pl.ds(pl.multiple_of(r * rows + shift, rows), rows)
            wb_ref[dst, :] = w_lo_ref[src, :].astype(BF16)
            return carry

        lax.fori_loop(0, (tn - shift) // rows, body, 0)
        if shift:
            wb_ref[tn - shift:tn, :] = w_hi_ref[...].astype(BF16)

    acc = lax.dot_general(h_ref[...], wb_ref[...], NT_DIMS, preferred_element_type=F32)
    if heads:
        for hh in range(o_ref.shape[0]):
            o_ref[hh] = acc[:, hh * LANES:(hh + 1) * LANES].astype(o_ref.dtype)
    else:
        o_ref[...] = acc.astype(o_ref.dtype)


def in_proj(h, w_t, layer, row0, n, heads, name, tm=1024, tn=1024):
    seq, k = h.shape
    shift = row0 % tn
    assert shift % W_ROW_ALIGN == 0 and n % tn == 0 and w_t.shape[1] >= row0 + n
    lo0 = row0 // tn
    in_specs = [pl.BlockSpec((tm, k), lambda j, i: (i, 0)),
                pl.BlockSpec((None, tn, k), lambda j, i: (layer, lo0 + j, 0))]
    operands = [h, w_t]
    if shift:
        assert tn % shift == 0
        per = tn // shift
        in_specs.append(pl.BlockSpec((None, shift, k), lambda j, i: (layer, (lo0 + j + 1) * per, 0)))
        operands.append(w_t)
    if heads:
        hp = tn // LANES
        out_spec = pl.BlockSpec((hp, tm, LANES), lambda j, i: (j, i, 0))
        out_shape = jax.ShapeDtypeStruct((n // LANES, seq, LANES), BF16)
    else:
        out_spec = pl.BlockSpec((tm, tn), lambda j, i: (i, j))
        out_shape = jax.ShapeDtypeStruct((seq, n), BF16)
    return pl.pallas_call(
        functools.partial(_proj_kernel, shift=shift, heads=heads),
        grid=(n // tn, seq // tm),
        in_specs=in_specs,
        out_specs=out_spec,
        out_shape=out_shape,
        scratch_shapes=[pltpu.VMEM((tn, k), BF16)],
        compiler_params=_cparams(("arbitrary", "arbitrary")),
        name=name,
    )(*operands)


def _ssd_kernel(p_ref, h0_ref, hn_ref, wdt_ref, cw_ref, cb_ref, dtb_ref, alog_ref, dsk_ref, nw_ref, o_ref,
                hbuf, xbc, state, ybuf, wdt_b, chain):
    t = SSD_T
    hist = SUBLANES
    c = pl.program_id(0)
    row = lax.broadcasted_iota(jnp.int32, (t, t), 0)
    col = lax.broadcasted_iota(jnp.int32, (t, t), 1)
    tril = row >= col
    tril_b = jnp.where(tril, 1.0, 0.0).astype(BF16)
    nxt = {}

    def chain_project(h_in):
        nxt["dt_raw"] = lax.dot_general(h_in[...], wdt_b[...], NT_DIMS, preferred_element_type=F32)

    def chain_cumsum():
        dt = _softplus(nxt["dt_raw"] + dtb_ref[...])
        da = dt * -jnp.exp(alog_ref[...])
        a_cum = None
        for part in _split_bf16(da, 3):
            term = jnp.dot(tril_b, part, preferred_element_type=F32)
            a_cum = term if a_cum is None else a_cum + term
        nxt["dt"], nxt["a_cum"] = dt, a_cum

    def chain_transpose():
        nxt["a_cum_t"] = nxt["a_cum"].T
        nxt["dt_t"] = nxt["dt"].T

    def chain_store():
        a_cum_t, dt_t = nxt["a_cum_t"], nxt["dt_t"]
        a_last_t = jnp.broadcast_to(a_cum_t[:, t - 1:t], a_cum_t.shape)
        chain[0] = nxt["a_cum"]
        chain[1] = a_cum_t
        chain[2] = dt_t
        chain[3] = jnp.exp(a_last_t - a_cum_t) * dt_t

    chain_stages = (chain_cumsum, chain_transpose, chain_store)

    @pl.when(c == 0)
    def _():
        hbuf[...] = jnp.zeros_like(hbuf)
        state[...] = jnp.zeros_like(state)
        wdt_b[0:SSD_HEADS, :] = wdt_ref[...].astype(BF16)
        wdt_b[SSD_HEADS:LANES, :] = jnp.zeros((LANES - SSD_HEADS, D_MODEL), BF16)
        chain_project(h0_ref)
        for stage in chain_stages:
            stage()

    a_cum = chain[0]
    a_cum_t = chain[1]
    dt_t = chain[2]
    w_t = chain[3]
    chunk_decay = jnp.exp(a_cum[t - 1:t, :])

    shift_by = {back: jnp.where(row - col == back, 1.0, 0.0).astype(BF16) for back in range(1, SSD_CONV)}
    top = lax.broadcasted_iota(jnp.int32, (hist, CONV_BLK), 0)

    def conv_blocks(j0, j1):
        for j in range(j0, j1):
            cs = slice(j * CONV_BLK, (j + 1) * CONV_BLK)
            ub = p_ref[:, SSD_WIDTH + j * CONV_BLK:SSD_WIDTH + (j + 1) * CONV_BLK]
            u = ub.astype(F32)
            prev = hbuf[:, cs]
            acc = cb_ref[:, cs] + cw_ref[SSD_CONV - 1:SSD_CONV, cs] * u
            for back in range(1, SSD_CONV):
                sh = jnp.dot(shift_by[back], ub, preferred_element_type=F32)
                head = jnp.where(top < back, pltpu.roll(prev, back, axis=0), sh[0:hist])
                sh = jnp.concatenate([head, sh[hist:]], axis=0)
                acc = acc + cw_ref[SSD_CONV - 1 - back:SSD_CONV - back, cs] * sh
            xbc[:, cs] = _silu(acc)
            hbuf[:, cs] = u[t - hist:t]

    conv_blocks(0, CONV_DIM // CONV_BLK)
    lane = lax.broadcasted_iota(jnp.int32, (t, LANES), 1)
    low_half = lane < SSD_HEAD_DIM

    heads_per_group = SSD_HEADS // SSD_GROUPS
    pairs_per_group = heads_per_group // 2
    n_pairs = SSD_HEADS // 2
    group_vals = {}
    prepared = {}
    issued = {}

    def pair_prepare(p):
        g = p // pairs_per_group
        if p % pairs_per_group == 0:
            b_g = xbc[:, SSD_WIDTH + g * SSD_STATE:SSD_WIDTH + (g + 1) * SSD_STATE]
            c_g = xbc[:, SSD_WIDTH + SSD_BC + g * SSD_STATE:SSD_WIDTH + SSD_BC + (g + 1) * SSD_STATE]
            c_gb = c_g.astype(BF16)
            cb = lax.dot_general(c_gb, b_g.astype(BF16), NT_DIMS, preferred_element_type=F32)
            group_vals[g] = (c_gb, cb, b_g.T)
        c_gb = group_vals[g][0]
        ps = slice(2 * p * SSD_HEAD_DIM, (2 * p + 2) * SSD_HEAD_DIM)
        x_pair = xbc[:, ps]
        x_lo = jnp.where(low_half, x_pair, 0.0).astype(BF16)
        x_hi = jnp.where(low_half, 0.0, x_pair).astype(BF16)
        off = jnp.dot(c_gb, state[:, ps].astype(BF16), preferred_element_type=F32)
        prepared[p] = (x_lo, x_hi, off, dsk_ref[:, ps] * x_pair)

    def pair_issue(p):
        _, cb, b_gt = group_vals[p // pairs_per_group]
        h0 = 2 * p
        x_lo, x_hi, off, skip = prepared.pop(p)
        yds, sns, ecols = [], [], []
        for hh, x_m in ((h0, x_lo), (h0 + 1, x_hi)):
            colv = jnp.broadcast_to(a_cum[:, hh:hh + 1], (t, t))
            rowv = a_cum_t[hh:hh + 1, :]
            decay = jnp.where(tril, jnp.exp(colv - rowv), 0.0)
            m = (cb * decay * dt_t[hh:hh + 1, :]).astype(BF16)
            yds.append(jnp.dot(m, x_m, preferred_element_type=F32))
            wm = (b_gt * w_t[hh:hh + 1, :]).astype(BF16)
            sns.append(jnp.dot(wm, x_m, preferred_element_type=F32))
            ecols.append(jnp.exp(colv))
        e_pair = jnp.where(low_half, ecols[0], ecols[1])
        issued[p] = (yds, sns, e_pair, off, skip)

    def pair_finish(p):
        yds, sns, e_pair, off, skip = issued.pop(p)
        h0 = 2 * p
        ps = slice(h0 * SSD_HEAD_DIM, (h0 + 2) * SSD_HEAD_DIM)
        ybuf[:, ps] = yds[0] + yds[1] + e_pair * off + skip
        cd = jnp.where(low_half[0:1, :], chunk_decay[:, h0:h0 + 1], chunk_decay[:, h0 + 1:h0 + 2])
        state[:, ps] = state[:, ps] * cd + (sns[0] + sns[1])

    def gate_group(g):
        gwid = SSD_WIDTH // SSD_GROUPS
        rows = 32
        gs = slice(g * gwid, (g + 1) * gwid)
        for r in range(t // rows):
            rs = slice(r * rows, (r + 1) * rows)
            yg = ybuf[rs, gs] * _silu(p_ref[rs, gs].astype(F32))
            ms = jnp.mean(yg * yg, axis=-1, keepdims=True)
            o_ref[rs, gs] = (yg * lax.rsqrt(ms + EPS) * nw_ref[:, gs]).astype(o_ref.dtype)

    for p in range(n_pairs):
        pair_prepare(p)
    chain_ticks = {SSD_CHAIN_TICKS[0]: lambda: chain_project(hn_ref)}
    chain_ticks.update(zip(SSD_CHAIN_TICKS[1:], chain_stages))
    for tick in range(n_pairs + SSD_PAIR_SKEW):
        if tick in chain_ticks:
            chain_ticks[tick]()
        if tick < n_pairs:
            pair_issue(tick)
        done = tick - SSD_PAIR_SKEW
        if done >= 0:
            pair_finish(done)
            if done % pairs_per_group == pairs_per_group - 1:
                gate_group(done // pairs_per_group)


def ssd_group(p_ssd, h, layer, w_t, conv_w, conv_b, dtb, alog, dsk, norm_w):
    seq = p_ssd.shape[0]
    t = SSD_T
    assert SSD_COLS % SSD_HEADS == 0
    per_layer = lambda *shape: pl.BlockSpec((None,) + shape, lambda i: (layer,) + (0,) * len(shape))
    return pl.pallas_call(
        _ssd_kernel,
        grid=(seq // t,),
        in_specs=[pl.BlockSpec((t, SSD_COLS), lambda i: (i, 0)),
                  pl.BlockSpec((t, D_MODEL), lambda i: (0, 0)),
                  pl.BlockSpec((t, D_MODEL), lambda i: (jnp.minimum(i + 1, seq // t - 1), 0)),
                  pl.BlockSpec((None, SSD_HEADS, D_MODEL), lambda i: (layer, SSD_COLS // SSD_HEADS, 0)),
                  per_layer(SSD_CONV, CONV_DIM),
                  per_layer(1, CONV_DIM),
                  per_layer(1, LANES),
                  per_layer(1, LANES),
                  per_layer(1, SSD_WIDTH),
                  per_layer(1, SSD_WIDTH)],
        out_specs=pl.BlockSpec((t, SSD_WIDTH), lambda i: (i, 0)),
        out_shape=jax.ShapeDtypeStruct((seq, SSD_WIDTH), BF16),
        scratch_shapes=[pltpu.VMEM((SUBLANES, CONV_DIM), F32),
                        pltpu.VMEM((t, CONV_DIM), F32),
                        pltpu.VMEM((SSD_STATE, SSD_WIDTH), F32),
                        pltpu.VMEM((t, SSD_WIDTH), F32),
                        pltpu.VMEM((LANES, D_MODEL), BF16),
                        pltpu.VMEM((4, t, LANES), F32)],
        compiler_params=_cparams(("arbitrary",)),
        name="ssd_group",
    )(p_ssd, h, h, w_t, conv_w, conv_b, dtb, alog, dsk, norm_w)


def _sba_kernel(q_ref, k_ref, v_ref, g_ref, o_ref, c_ref, acc_ref):
    rows = SBA_ROWS
    kw = SBA_KEYS
    qt = q_ref.shape[1]
    n_sub = qt // rows
    subs = range(n_sub)
    qi = pl.program_id(1)
    scale = 1.0 / math.sqrt(SBA_DIM)
    row = lax.broadcasted_iota(jnp.int32, (rows, kw), 0)
    col = lax.broadcasted_iota(jnp.int32, (rows, kw), 1)
    col_minus_row = col - row
    r2 = lax.broadcasted_iota(jnp.int32, (SBA_SUM_PARTS * kw, 2 * kw), 0) % kw
    c2 = lax.broadcasted_iota(jnp.int32, (SBA_SUM_PARTS * kw, 2 * kw), 1)
    neg_cum = jnp.where((c2 >= kw) | (r2 > c2), -1.0, 0.0).astype(BF16)

    def sweep(ds, opening):
        grp = SBA_CUM_GROUP
        units = [(i, g0) for g0 in range(0, n_sub, grp) for i in range(len(ds))]
        last_i = len(ds) - 1
        zs, masks, sps, sums, ws_b, kss, starts = {}, {}, {}, {}, {}, {}, {}
        carries = {}
        wants = []

        def stage_scores(i, g0):
            for s in range(g0, g0 + grp):
                q0 = qi * qt + s * rows
                start = q0 + rows - (ds[i] + 1) * kw
                load = jnp.maximum(start, 0)
                starts[i, s] = start
                kss[i, s] = pl.ds(pl.multiple_of(load, rows), kw)
                zs[i, s] = lax.dot_general(q_ref[0, s * rows:(s + 1) * rows, :], k_ref[0, kss[i, s], :], NT_DIMS,
                                           preferred_element_type=F32) * scale
                if opening and i == 0:
                    masks[i, s] = col_minus_row < (q0 - load)
                else:
                    masks[i, s] = col < (start + kw - load)

        def stage_sums(i, g0):
            parts = []
            for s in range(g0, g0 + grp):
                sps[i, s] = jnp.where(masks[i, s], _softplus(zs[i, s]), 0.0)
                parts.append(jnp.concatenate(_split_bf16(sps[i, s], SBA_SUM_PARTS), axis=1))
            both = jnp.dot(jnp.concatenate(parts, axis=0), neg_cum, preferred_element_type=F32)
            for j, s in enumerate(range(g0, g0 + grp)):
                sums[i, s] = both[j * rows:(j + 1) * rows]

        def stage_weights(i, g0):
            for s in range(g0, g0 + grp):
                if i == 0:
                    carries[s] = None if opening else c_ref[s]
                logw = zs[i, s] - sps[i, s] + sums[i, s][:, :kw]
                if carries[s] is not None:
                    logw = logw + carries[s]
                ws_b[i, s] = jnp.where(masks[i, s], jnp.exp(logw), 0.0).astype(BF16)
                tot = sums[i, s][:, kw:]
                carries[s] = tot if carries[s] is None else carries[s] + tot
                if i == last_i:
                    c_ref[s] = carries[s]
                    wants.append(jnp.where(starts[i, s] > 0, carries[s], -jnp.inf))

        def stage_values(i, g0):
            for s in range(g0, g0 + grp):
                pv = jnp.dot(ws_b[i, s], v_ref[0, kss[i, s], :], preferred_element_type=F32)
                if opening and i == 0:
                    acc_ref[s] = pv
                else:
                    acc_ref[s] += pv

        stages = (stage_scores, stage_sums, stage_weights, stage_values)
        skew = SBA_STAGE_SKEW
        for tick in range(len(units) + (len(stages) - 1) * skew):
            for j in reversed(range(len(stages))):
                u = tick - j * skew
                if 0 <= u < len(units):
                    stages[j](*units[u])
        pending = wants[0]
        for want in wants[1:]:
            pending = jnp.maximum(pending, want)
        return jnp.max(pending)

    n_open = SBA_OPENING_STEPS
    first = sweep(list(range(n_open)), True)
    lax.while_loop(lambda st: st[1] > SBA_EXIT,
                   lambda st: (st[0] + 1, sweep([st[0]], False)),
                   (jnp.int32(n_open), first))
    for s in subs:
        qs = slice(s * rows, (s + 1) * rows)
        o_ref[qs, :] = (acc_ref[s] * _silu(g_ref[0, qs, :].astype(F32))).astype(o_ref.dtype)


def sba_group(p_sba):
    seq = p_sba.shape[1]
    qt = min(SBA_QT, seq)
    nh = SBA_HEADS
    return pl.pallas_call(
        _sba_kernel,
        grid=(nh, seq // qt),
        in_specs=[pl.BlockSpec((1, qt, SBA_DIM), lambda h, i: (h, i, 0)),
                  pl.BlockSpec((1, seq, SBA_DIM), lambda h, i: (nh + h, 0, 0)),
                  pl.BlockSpec((1, seq, SBA_DIM), lambda h, i: (2 * nh + h, 0, 0)),
                  pl.BlockSpec((1, qt, SBA_DIM), lambda h, i: (3 * nh + h, i, 0))],
        out_specs=pl.BlockSpec((qt, SBA_DIM), lambda h, i: (i, h)),
        out_shape=jax.ShapeDtypeStruct((seq, SBA_WIDTH), BF16),
        scratch_shapes=[pltpu.VMEM((qt // SBA_ROWS, SBA_ROWS, SBA_KEYS), F32),
                        pltpu.VMEM((qt // SBA_ROWS, SBA_ROWS, SBA_DIM), F32)],
        compiler_params=_cparams(("arbitrary", "arbitrary")),
        name="sba_group",
    )(p_sba, p_sba, p_sba, p_sba)


def _out_proj_kernel(ya_ref, yb_ref, wa_ref, wb_ref, x_ref, nw_ref, *out_refs, emit_x):
    acc = jnp.dot(ya_ref[...], wa_ref[...], preferred_element_type=F32)
    acc = acc + jnp.dot(yb_ref[...], wb_ref[...], preferred_element_type=F32)
    x_new = x_ref[...] + acc
    if emit_x:
        xo_ref, h_ref = out_refs
        xo_ref[...] = x_new
    else:
        (h_ref,) = out_refs
    h_ref[...] = _rmsnorm_rows(x_new, nw_ref[...]).astype(h_ref.dtype)


def out_proj(y_ssd, y_sba, w_all, layer, x, norm_all, norm_idx, last, tm=256):
    seq, d = x.shape
    ka, kb = y_ssd.shape[1], y_sba.shape[1]
    assert ka == kb and w_all.shape[1] == ka + kb
    row_spec = lambda width: pl.BlockSpec((tm, width), lambda i: (i, 0))
    if last:
        out_specs = row_spec(d)
        out_shape = jax.ShapeDtypeStruct((seq, d), F32)
    else:
        out_specs = [row_spec(d), row_spec(d)]
        out_shape = [jax.ShapeDtypeStruct((seq, d), F32), jax.ShapeDtypeStruct((seq, d), BF16)]
    return pl.pallas_call(
        functools.partial(_out_proj_kernel, emit_x=not last),
        grid=(seq // tm,),
        in_specs=[row_spec(ka), row_spec(kb),
                  pl.BlockSpec((None, ka, d), lambda i: (layer, 0, 0)),
                  pl.BlockSpec((None, kb, d), lambda i: (layer, 1, 0)),
                  row_spec(d),
                  pl.BlockSpec((None, 1, d), lambda i: (norm_idx, 0, 0))],
        out_specs=out_specs,
        out_shape=out_shape,
        compiler_params=_cparams(("arbitrary",)),
        name="out_proj",
    )(y_ssd, y_sba, w_all, w_all, x, norm_all)


def kernel(x, norm_w, w_in, conv_w, conv_b, dt_bias, a_log, d_skip, ssd_norm_w, w_out, final_norm_w):
    bsz, seq, d = x.shape
    assert bsz == 1 and d == D_MODEL
    depth = w_in.shape[0]
    sba_row0 = SSD_COLS + SSD_HEADS
    pad = LANES - SSD_HEADS
    w_in_t = jnp.swapaxes(w_in, 1, 2)
    w_out_b = w_out.astype(BF16)
    dtb = jnp.pad(dt_bias, ((0, 0), (0, pad))).reshape(depth, 1, LANES)
    alog = jnp.pad(a_log, ((0, 0), (0, pad))).reshape(depth, 1, LANES)
    dsk = jnp.repeat(d_skip, SSD_HEAD_DIM, axis=1).reshape(depth, 1, SSD_WIDTH)
    conv_b3 = conv_b.reshape(depth, 1, CONV_DIM)
    ssd_nw = ssd_norm_w.reshape(depth, 1, SSD_WIDTH)
    norms = jnp.concatenate([norm_w, final_norm_w[None]], axis=0).reshape(depth + 1, 1, d)

    xs = x.reshape(seq, d)
    h = rmsnorm_bf16(xs, norm_w[0])
    for i in range(depth):
        p_ssd = in_proj(h, w_in_t, i, 0, SSD_COLS, False, "in_proj_ssd")
        p_sba = in_proj(h, w_in_t, i, sba_row0, 4 * SBA_WIDTH, True, "in_proj_sba")
        y_ssd = ssd_group(p_ssd, h, i, w_in_t, conv_w, conv_b3, dtb, alog, dsk, ssd_nw)
        y_sba = sba_group(p_sba)
        last = i == depth - 1
        res = out_proj(y_ssd, y_sba, w_out_b, i, xs, norms, i + 1, last)
        if last:
            out = res
        else:
            xs, h = res
    return out.reshape(bsz, seq, d)
```

```python
import functools
import math

import jax
import jax.numpy as jnp
from jax import lax
from jax.experimental import pallas as pl
from jax.experimental.pallas import tpu as pltpu

F32 = jnp.float32
BF16 = jnp.bfloat16

D_MODEL = 2048
SSD_WIDTH = 2048
SSD_HEAD_DIM = 64
SSD_HEADS = SSD_WIDTH // SSD_HEAD_DIM
SSD_GROUPS = 4
SSD_STATE = 128
SSD_CONV = 4
SSD_BC = SSD_GROUPS * SSD_STATE
CONV_DIM = SSD_WIDTH + 2 * SSD_BC
SSD_COLS = SSD_WIDTH + CONV_DIM
SBA_HEADS = 16
SBA_DIM = 128
SBA_WIDTH = SBA_HEADS * SBA_DIM
EPS = 1e-6

LANES = 128
SUBLANES = 8
VMEM_LIMIT = 56 * 1024 * 1024

SSD_T = 128
CONV_BLK = 256
SSD_PAIR_SKEW = 2
SSD_CHAIN_TICKS = (1, 5, 10, 16)
SBA_ROWS = 64
SBA_KEYS = 128
SBA_CUM_GROUP = 2
SBA_SUM_PARTS = 1
SBA_OPENING_STEPS = 2
SBA_STAGE_SKEW = 6
SBA_QT = 4096
SBA_EXIT = -104.0


def _cparams(sem):
    return pltpu.CompilerParams(dimension_semantics=sem, vmem_limit_bytes=VMEM_LIMIT)


def _sigmoid(a):
    return 1.0 / (1.0 + jnp.exp(-a))


def _silu(a):
    return a * _sigmoid(a)


def _softplus(a):
    return jnp.maximum(a, 0.0) + jnp.log(1.0 + jnp.exp(-jnp.abs(a)))


def _split_bf16(a, parts):
    out = []
    r = a
    for i in range(parts):
        p = r.astype(BF16)
        out.append(p)
        if i + 1 < parts:
            r = r - p.astype(F32)
    return out


def _rmsnorm_rows(x, w):
    ms = jnp.mean(x * x, axis=-1, keepdims=True)
    return x * lax.rsqrt(ms + EPS) * w


def _rmsnorm_kernel(x_ref, w_ref, o_ref, *, rows):
    w = w_ref[...]

    def body(r, carry):
        sl = pl.ds(pl.multiple_of(r * rows, rows), rows)
        o_ref[sl, :] = _rmsnorm_rows(x_ref[sl, :], w).astype(o_ref.dtype)
        return carry

    lax.fori_loop(0, x_ref.shape[0] // rows, body, 0)


def rmsnorm_bf16(x, w, tm=256):
    seq, d = x.shape
    return pl.pallas_call(
        functools.partial(_rmsnorm_kernel, rows=32),
        grid=(seq // tm,),
        in_specs=[pl.BlockSpec((tm, d), lambda i: (i, 0)),
                  pl.BlockSpec((1, d), lambda i: (0, 0))],
        out_specs=pl.BlockSpec((tm, d), lambda i: (i, 0)),
        out_shape=jax.ShapeDtypeStruct((seq, d), BF16),
        compiler_params=_cparams(("arbitrary",)),
        name="rmsnorm",
    )(x, w.reshape(1, d))


NT_DIMS = (((1,), (1,)), ((), ()))
W_ROW_ALIGN = 32


def _proj_kernel(h_ref, *refs, shift, heads, scaled_tiles, scale):
    if shift:
        w_lo_ref, w_hi_ref, o_ref, wb_ref = refs
    else:
        w_lo_ref, o_ref, wb_ref = refs
    tn = wb_ref.shape[0]

    @pl.when(pl.program_id(1) == 0)
    def _():
        rows = W_ROW_ALIGN
        factor = jnp.where(pl.program_id(0) < scaled_tiles, scale, 1.0).astype(F32) if scaled_tiles else None
        stage = (lambda w: (w * factor).astype(BF16)) if scaled_tiles else (lambda w: w.astype(BF16))

        def body(r, carry):
            dst = pl.ds(pl.multiple_of(r * rows, rows), rows)
            src = pl.ds(pl.multiple_of(r * rows + shift, rows), rows)
            wb_ref[dst, :] = stage(w_lo_ref[src, :])
            return carry

        lax.fori_loop(0, (tn - shift) // rows, body, 0)
        if shift:
            wb_ref[tn - shift:tn, :] = stage(w_hi_ref[...])

    acc = lax.dot_general(h_ref[...], wb_ref[...], NT_DIMS, preferred_element_type=F32)
    if heads:
        for hh in range(o_ref.shape[0]):
            o_ref[hh] = acc[:, hh * LANES:(hh + 1) * LANES].astype(o_ref.dtype)
    else:
        o_ref[...] = acc.astype(o_ref.dtype)


def in_proj(h, w_t, layer, row0, n, heads, name, scaled_cols=0, scale=1.0, tm=1024, tn=1024):
    seq, k = h.shape
    shift = row0 % tn
    assert shift % W_ROW_ALIGN == 0 and n % tn == 0 and w_t.shape[1] >= row0 + n and scaled_cols % tn == 0
    lo0 = row0 // tn
    in_specs = [pl.BlockSpec((tm, k), lambda j, i: (i, 0)),
                pl.BlockSpec((None, tn, k), lambda j, i: (layer, lo0 + j, 0))]
    operands = [h, w_t]
    if shift:
        assert tn % shift == 0
        per = tn // shift
        in_specs.append(pl.BlockSpec((None, shift, k), lambda j, i: (layer, (lo0 + j + 1) * per, 0)))
        operands.append(w_t)
    if heads:
        hp = tn // LANES
        out_spec = pl.BlockSpec((hp, tm, LANES), lambda j, i: (j, i, 0))
        out_shape = jax.ShapeDtypeStruct((n // LANES, seq, LANES), BF16)
    else:
        out_spec = pl.BlockSpec((tm, tn), lambda j, i: (i, j))
        out_shape = jax.ShapeDtypeStruct((seq, n), BF16)
    return pl.pallas_call(
        functools.partial(_proj_kernel, shift=shift, heads=heads, scaled_tiles=scaled_cols // tn, scale=scale),
        grid=(n // tn, seq // tm),
        in_specs=in_specs,
        out_specs=out_spec,
        out_shape=out_shape,
        scratch_shapes=[pltpu.VMEM((tn, k), BF16)],
        compiler_params=_cparams(("arbitrary", "arbitrary")),
        name=name,
    )(*operands)


def _ssd_kernel(p_ref, h0_ref, hn_ref, wdt_ref, cw_ref, cb_ref, dtb_ref, alog_ref, dsk_ref, nw_ref, o_ref,
                hbuf, xbc, state, ybuf, wdt_b, chain):
    t = SSD_T
    hist = SUBLANES
    c = pl.program_id(0)
    row = lax.broadcasted_iota(jnp.int32, (t, t), 0)
    col = lax.broadcasted_iota(jnp.int32, (t, t), 1)
    tril = row >= col
    tril_b = jnp.where(tril, 1.0, 0.0).astype(BF16)
    nxt = {}

    def chain_project(h_in):
        nxt["dt_raw"] = lax.dot_general(h_in[...], wdt_b[...], NT_DIMS, preferred_element_type=F32)

    def chain_cumsum():
        dt = _softplus(nxt["dt_raw"] + dtb_ref[...])
        da = dt * -jnp.exp(alog_ref[...])
        a_cum = None
        for part in _split_bf16(da, 3):
            term = jnp.dot(tril_b, part, preferred_element_type=F32)
            a_cum = term if a_cum is None else a_cum + term
        nxt["dt"], nxt["a_cum"] = dt, a_cum

    def chain_transpose():
        nxt["a_cum_t"] = nxt["a_cum"].T
        nxt["dt_t"] = nxt["dt"].T

    def chain_store():
        a_cum_t, dt_t = nxt["a_cum_t"], nxt["dt_t"]
        a_last_t = jnp.broadcast_to(a_cum_t[:, t - 1:t], a_cum_t.shape)
        chain[0] = nxt["a_cum"]
        chain[1] = a_cum_t
        chain[2] = dt_t
        chain[3] = jnp.exp(a_last_t - a_cum_t) * dt_t

    chain_stages = (chain_cumsum, chain_transpose, chain_store)

    @pl.when(c == 0)
    def _():
        hbuf[...] = jnp.zeros_like(hbuf)
        state[...] = jnp.zeros_like(state)
        wdt_b[0:SSD_HEADS, :] = wdt_ref[...].astype(BF16)
        wdt_b[SSD_HEADS:LANES, :] = jnp.zeros((LANES - SSD_HEADS, D_MODEL), BF16)
        chain_project(h0_ref)
        for stage in chain_stages:
            stage()

    a_cum = chain[0]
    a_cum_t = chain[1]
    dt_t = chain[2]
    w_t = chain[3]
    chunk_decay = jnp.exp(a_cum[t - 1:t, :])

    shift_by = {back: jnp.where(row - col == back, 1.0, 0.0).astype(BF16) for back in range(1, SSD_CONV)}
    top = lax.broadcasted_iota(jnp.int32, (hist, CONV_BLK), 0)

    def conv_blocks(j0, j1):
        for j in range(j0, j1):
            cs = slice(j * CONV_BLK, (j + 1) * CONV_BLK)
            ub = p_ref[:, SSD_WIDTH + j * CONV_BLK:SSD_WIDTH + (j + 1) * CONV_BLK]
            u = ub.astype(F32)
            prev = hbuf[:, cs]
            acc = cb_ref[:, cs] + cw_ref[SSD_CONV - 1:SSD_CONV, cs] * u
            for back in range(1, SSD_CONV):
                sh = jnp.dot(shift_by[back], ub, preferred_element_type=F32)
                head = jnp.where(top < back, pltpu.roll(prev, back, axis=0), sh[0:hist])
                sh = jnp.concatenate([head, sh[hist:]], axis=0)
                acc = acc + cw_ref[SSD_CONV - 1 - back:SSD_CONV - back, cs] * sh
            xbc[:, cs] = _silu(acc)
            hbuf[:, cs] = u[t - hist:t]

    conv_blocks(0, CONV_DIM // CONV_BLK)
    lane = lax.broadcasted_iota(jnp.int32, (t, LANES), 1)
    low_half = lane < SSD_HEAD_DIM

    heads_per_group = SSD_HEADS // SSD_GROUPS
    pairs_per_group = heads_per_group // 2
    n_pairs = SSD_HEADS // 2
    group_vals = {}
    prepared = {}
    issued = {}

    def pair_prepare(p):
        g = p // pairs_per_group
        if p % pairs_per_group == 0:
            b_g = xbc[:, SSD_WIDTH + g * SSD_STATE:SSD_WIDTH + (g + 1) * SSD_STATE]
            c_g = xbc[:, SSD_WIDTH + SSD_BC + g * SSD_STATE:SSD_WIDTH + SSD_BC + (g + 1) * SSD_STATE]
            c_gb = c_g.astype(BF16)
            cb = lax.dot_general(c_gb, b_g.astype(BF16), NT_DIMS, preferred_element_type=F32)
            group_vals[g] = (c_gb, cb, b_g.T)
        c_gb = group_vals[g][0]
        ps = slice(2 * p * SSD_HEAD_DIM, (2 * p + 2) * SSD_HEAD_DIM)
        x_pair = xbc[:, ps]
        x_lo = jnp.where(low_half, x_pair, 0.0).astype(BF16)
        x_hi = jnp.where(low_half, 0.0, x_pair).astype(BF16)
        off = jnp.dot(c_gb, state[:, ps].astype(BF16), preferred_element_type=F32)
        prepared[p] = (x_lo, x_hi, off, dsk_ref[:, ps] * x_pair)

    def pair_issue(p):
        _, cb, b_gt = group_vals[p // pairs_per_group]
        h0 = 2 * p
        x_lo, x_hi, off, skip = prepared.pop(p)
        yds, sns, ecols = [], [], []
        for hh, x_m in ((h0, x_lo), (h0 + 1, x_hi)):
            colv = jnp.broadcast_to(a_cum[:, hh:hh + 1], (t, t))
            rowv = a_cum_t[hh:hh + 1, :]
            decay = jnp.where(tril, jnp.exp(colv - rowv), 0.0)
            m = (cb * decay * dt_t[hh:hh + 1, :]).astype(BF16)
            yds.append(jnp.dot(m, x_m, preferred_element_type=F32))
            wm = (b_gt * w_t[hh:hh + 1, :]).astype(BF16)
            sns.append(jnp.dot(wm, x_m, preferred_element_type=F32))
            ecols.append(jnp.exp(colv))
        e_pair = jnp.where(low_half, ecols[0], ecols[1])
        issued[p] = (yds, sns, e_pair, off, skip)

    def pair_finish(p):
        yds, sns, e_pair, off, skip = issued.pop(p)
        h0 = 2 * p
        ps = slice(h0 * SSD_HEAD_DIM, (h0 + 2) * SSD_HEAD_DIM)
        ybuf[:, ps] = yds[0] + yds[1] + e_pair * off + skip
        cd = jnp.where(low_half[0:1, :], chunk_decay[:, h0:h0 + 1], chunk_decay[:, h0 + 1:h0 + 2])
        state[:, ps] = state[:, ps] * cd + (sns[0] + sns[1])

    def gate_group(g):
        gwid = SSD_WIDTH // SSD_GROUPS
        rows = 32
        gs = slice(g * gwid, (g + 1) * gwid)
        for r in range(t // rows):
            rs = slice(r * rows, (r + 1) * rows)
            yg = ybuf[rs, gs] * _silu(p_ref[rs, gs].astype(F32))
            ms = jnp.mean(yg * yg, axis=-1, keepdims=True)
            o_ref[rs, gs] = (yg * lax.rsqrt(ms + EPS) * nw_ref[:, gs]).astype(o_ref.dtype)

    for p in range(n_pairs):
        pair_prepare(p)
    chain_ticks = {SSD_CHAIN_TICKS[0]: lambda: chain_project(hn_ref)}
    chain_ticks.update(zip(SSD_CHAIN_TICKS[1:], chain_stages))
    for tick in range(n_pairs + SSD_PAIR_SKEW):
        if tick in chain_ticks:
            chain_ticks[tick]()
        if tick < n_pairs:
            pair_issue(tick)
        done = tick - SSD_PAIR_SKEW
        if done >= 0:
            pair_finish(done)
            if done % pairs_per_group == pairs_per_group - 1:
                gate_group(done // pairs_per_group)


def ssd_group(p_ssd, h, layer, w_t, conv_w, conv_b, dtb, alog, dsk, norm_w):
    seq = p_ssd.shape[0]
    t = SSD_T
    assert SSD_COLS % SSD_HEADS == 0
    per_layer = lambda *shape: pl.BlockSpec((None,) + shape, lambda i: (layer,) + (0,) * len(shape))
    return pl.pallas_call(
        _ssd_kernel,
        grid=(seq // t,),
        in_specs=[pl.BlockSpec((t, SSD_COLS), lambda i: (i, 0)),
                  pl.BlockSpec((t, D_MODEL), lambda i: (0, 0)),
                  pl.BlockSpec((t, D_MODEL), lambda i: (jnp.minimum(i + 1, seq // t - 1), 0)),
                  pl.BlockSpec((None, SSD_HEADS, D_MODEL), lambda i: (layer, SSD_COLS // SSD_HEADS, 0)),
                  per_layer(SSD_CONV, CONV_DIM),
                  per_layer(1, CONV_DIM),
                  per_layer(1, LANES),
                  per_layer(1, LANES),
                  per_layer(1, SSD_WIDTH),
                  per_layer(1, SSD_WIDTH)],
        out_specs=pl.BlockSpec((t, SSD_WIDTH), lambda i: (i, 0)),
        out_shape=jax.ShapeDtypeStruct((seq, SSD_WIDTH), BF16),
        scratch_shapes=[pltpu.VMEM((SUBLANES, CONV_DIM), F32),
                        pltpu.VMEM((t, CONV_DIM), F32),
                        pltpu.VMEM((SSD_STATE, SSD_WIDTH), F32),
                        pltpu.VMEM((t, SSD_WIDTH), F32),
                        pltpu.VMEM((LANES, D_MODEL), BF16),
                        pltpu.VMEM((4, t, LANES), F32)],
        compiler_params=_cparams(("arbitrary",)),
        name="ssd_group",
    )(p_ssd, h, h, w_t, conv_w, conv_b, dtb, alog, dsk, norm_w)


def _sba_kernel(q_ref, k_ref, v_ref, g_ref, o_ref, c_ref, acc_ref):
    rows = SBA_ROWS
    kw = SBA_KEYS
    qt = q_ref.shape[1]
    n_sub = qt // rows
    subs = range(n_sub)
    qi = pl.program_id(1)
    row = lax.broadcasted_iota(jnp.int32, (rows, kw), 0)
    col = lax.broadcasted_iota(jnp.int32, (rows, kw), 1)
    col_minus_row = col - row
    r2 = lax.broadcasted_iota(jnp.int32, (SBA_SUM_PARTS * kw, 2 * kw), 0) % kw
    c2 = lax.broadcasted_iota(jnp.int32, (SBA_SUM_PARTS * kw, 2 * kw), 1)
    neg_cum = jnp.where((c2 >= kw) | (r2 > c2), -1.0, 0.0).astype(BF16)

    def sweep(ds, opening):
        grp = SBA_CUM_GROUP
        units = [(i, g0) for g0 in range(0, n_sub, grp) for i in range(len(ds))]
        last_i = len(ds) - 1
        zs, masks, sps, sums, ws_b, kss, starts = {}, {}, {}, {}, {}, {}, {}
        carries = {}
        wants = []

        def stage_scores(i, g0):
            for s in range(g0, g0 + grp):
                q0 = qi * qt + s * rows
                start = q0 + rows - (ds[i] + 1) * kw
                load = jnp.maximum(start, 0)
                starts[i, s] = start
                kss[i, s] = pl.ds(pl.multiple_of(load, rows), kw)
                zs[i, s] = lax.dot_general(q_ref[0, s * rows:(s + 1) * rows, :], k_ref[0, kss[i, s], :], NT_DIMS,
                                           preferred_element_type=F32)
                if opening and i == 0:
                    masks[i, s] = col_minus_row < (q0 - load)
                else:
                    masks[i, s] = col < (start + kw - load)

        def stage_sums(i, g0):
            parts = []
            for s in range(g0, g0 + grp):
                sps[i, s] = jnp.where(masks[i, s], _softplus(zs[i, s]), 0.0)
                parts.append(jnp.concatenate(_split_bf16(sps[i, s], SBA_SUM_PARTS), axis=1))
            both = jnp.dot(jnp.concatenate(parts, axis=0), neg_cum, preferred_element_type=F32)
            for j, s in enumerate(range(g0, g0 + grp)):
                sums[i, s] = both[j * rows:(j + 1) * rows]

        def stage_weights(i, g0):
            for s in range(g0, g0 + grp):
                if i == 0:
                    carries[s] = None if opening else c_ref[s]
                logw = zs[i, s] - sps[i, s] + sums[i, s][:, :kw]
                if carries[s] is not None:
                    logw = logw + carries[s]
                ws_b[i, s] = jnp.where(masks[i, s], jnp.exp(logw), 0.0).astype(BF16)
                tot = sums[i, s][:, kw:]
                carries[s] = tot if carries[s] is None else carries[s] + tot
                if i == last_i:
                    c_ref[s] = carries[s]
                    wants.append(jnp.where(starts[i, s] > 0, carries[s], -jnp.inf))

        def stage_values(i, g0):
            for s in range(g0, g0 + grp):
                pv = jnp.dot(ws_b[i, s], v_ref[0, kss[i, s], :], preferred_element_type=F32)
                if opening and i == 0:
                    acc_ref[s] = pv
                else:
                    acc_ref[s] += pv

        stages = (stage_scores, stage_sums, stage_weights, stage_values)
        skew = SBA_STAGE_SKEW
        for tick in range(len(units) + (len(stages) - 1) * skew):
            for j in reversed(range(len(stages))):
                u = tick - j * skew
                if 0 <= u < len(units):
                    stages[j](*units[u])
        pending = wants[0]
        for want in wants[1:]:
            pending = jnp.maximum(pending, want)
        return jnp.max(pending)

    n_open = SBA_OPENING_STEPS
    first = sweep(list(range(n_open)), True)
    lax.while_loop(lambda st: st[1] > SBA_EXIT,
                   lambda st: (st[0] + 1, sweep([st[0]], False)),
                   (jnp.int32(n_open), first))
    for s in subs:
        qs = slice(s * rows, (s + 1) * rows)
        o_ref[qs, :] = (acc_ref[s] * _silu(g_ref[0, qs, :].astype(F32))).astype(o_ref.dtype)


def sba_group(p_sba):
    seq = p_sba.shape[1]
    qt = min(SBA_QT, seq)
    nh = SBA_HEADS
    return pl.pallas_call(
        _sba_kernel,
        grid=(nh, seq // qt),
        in_specs=[pl.BlockSpec((1, qt, SBA_DIM), lambda h, i: (h, i, 0)),
                  pl.BlockSpec((1, seq, SBA_DIM), lambda h, i: (nh + h, 0, 0)),
                  pl.BlockSpec((1, seq, SBA_DIM), lambda h, i: (2 * nh + h, 0, 0)),
                  pl.BlockSpec((1, qt, SBA_DIM), lambda h, i: (3 * nh + h, i, 0))],
        out_specs=pl.BlockSpec((qt, SBA_DIM), lambda h, i: (i, h)),
        out_shape=jax.ShapeDtypeStruct((seq, SBA_WIDTH), BF16),
        scratch_shapes=[pltpu.VMEM((qt // SBA_ROWS, SBA_ROWS, SBA_KEYS), F32),
                        pltpu.VMEM((qt // SBA_ROWS, SBA_ROWS, SBA_DIM), F32)],
        compiler_params=_cparams(("arbitrary", "arbitrary")),
        name="sba_group",
    )(p_sba, p_sba, p_sba, p_sba)


def _out_proj_kernel(ya_ref, yb_ref, wa_ref, wb_ref, x_ref, nw_ref, *out_refs, emit_x):
    acc = jnp.dot(ya_ref[...], wa_ref[...], preferred_element_type=F32)
    acc = acc + jnp.dot(yb_ref[...], wb_ref[...], preferred_element_type=F32)
    x_new = x_ref[...] + acc
    if emit_x:
        xo_ref, h_ref = out_refs
        xo_ref[...] = x_new
    else:
        (h_ref,) = out_refs
    h_ref[...] = _rmsnorm_rows(x_new, nw_ref[...]).astype(h_ref.dtype)


def out_proj(y_ssd, y_sba, w_all, layer, x, norm_all, norm_idx, last, tm=256):
    seq, d = x.shape
    ka, kb = y_ssd.shape[1], y_sba.shape[1]
    assert ka == kb and w_all.shape[1] == ka + kb
    row_spec = lambda width: pl.BlockSpec((tm, width), lambda i: (i, 0))
    if last:
        out_specs = row_spec(d)
        out_shape = jax.ShapeDtypeStruct((seq, d), F32)
    else:
        out_specs = [row_spec(d), row_spec(d)]
        out_shape = [jax.ShapeDtypeStruct((seq, d), F32), jax.ShapeDtypeStruct((seq, d), BF16)]
    return pl.pallas_call(
        functools.partial(_out_proj_kernel, emit_x=not last),
        grid=(seq // tm,),
        in_specs=[row_spec(ka), row_spec(kb),
                  pl.BlockSpec((None, ka, d), lambda i: (layer, 0, 0)),
                  pl.BlockSpec((None, kb, d), lambda i: (layer, 1, 0)),
                  row_spec(d),
                  pl.BlockSpec((None, 1, d), lambda i: (norm_idx, 0, 0))],
        out_specs=out_specs,
        out_shape=out_shape,
        compiler_params=_cparams(("arbitrary",)),
        name="out_proj",
    )(y_ssd, y_sba, w_all, w_all, x, norm_all)


def kernel(x, norm_w, w_in, conv_w, conv_b, dt_bias, a_log, d_skip, ssd_norm_w, w_out, final_norm_w):
    bsz, seq, d = x.shape
    assert bsz == 1 and d == D_MODEL
    depth = w_in.shape[0]
    sba_row0 = SSD_COLS + SSD_HEADS
    pad = LANES - SSD_HEADS
    w_in_t = jnp.swapaxes(w_in, 1, 2)
    w_out_b = w_out.astype(BF16)
    dtb = jnp.pad(dt_bias, ((0, 0), (0, pad))).reshape(depth, 1, LANES)
    alog = jnp.pad(a_log, ((0, 0), (0, pad))).reshape(depth, 1, LANES)
    dsk = jnp.repeat(d_skip, SSD_HEAD_DIM, axis=1).reshape(depth, 1, SSD_WIDTH)
    conv_b3 = conv_b.reshape(depth, 1, CONV_DIM)
    ssd_nw = ssd_norm_w.reshape(depth, 1, SSD_WIDTH)
    norms = jnp.concatenate([norm_w, final_norm_w[None]], axis=0).reshape(depth + 1, 1, d)

    xs = x.reshape(seq, d)
    h = rmsnorm_bf16(xs, norm_w[0])
    for i in range(depth):
        p_ssd = in_proj(h, w_in_t, i, 0, SSD_COLS, False, "in_proj_ssd")
        p_sba = in_proj(h, w_in_t, i, sba_row0, 4 * SBA_WIDTH, True, "in_proj_sba",
                        scaled_cols=SBA_WIDTH, scale=1.0 / math.sqrt(SBA_DIM))
        y_ssd = ssd_group(p_ssd, h, i, w_in_t, conv_w, conv_b3, dtb, alog, dsk, ssd_nw)
        y_sba = sba_group(p_sba)
        last = i == depth - 1
        res = out_proj(y_ssd, y_sba, w_out_b, i, xs, norms, i + 1, last)
        if last:
            out = res
        else:
            xs, h = res
    return out.reshape(bsz, seq, d)
```

```python
import functools
import math

import jax
import jax.numpy as jnp
from jax import lax
from jax.experimental import pallas as pl
from jax.experimental.pallas import tpu as pltpu

F32 = jnp.float32
BF16 = jnp.bfloat16

D_MODEL = 2048
SSD_WIDTH = 2048
SSD_HEAD_DIM = 64
SSD_HEADS = SSD_WIDTH // SSD_HEAD_DIM
SSD_GROUPS = 4
SSD_STATE = 128
SSD_CONV = 4
SSD_BC = SSD_GROUPS * SSD_STATE
CONV_DIM = SSD_WIDTH + 2 * SSD_BC
SSD_COLS = SSD_WIDTH + CONV_DIM
SBA_HEADS = 16
SBA_DIM = 128
SBA_WIDTH = SBA_HEADS * SBA_DIM
EPS = 1e-6

LANES = 128
SUBLANES = 8
VMEM_LIMIT = 56 * 1024 * 1024

SSD_T = 128
CONV_BLK = 256
SSD_PAIR_SKEW = 2
SSD_CHAIN_TICKS = (1, 5, 10, 16)
SBA_ROWS = 64
SBA_KEYS = 128
SBA_CUM_GROUP = 2
SBA_SUM_PARTS = 1
SBA_OPENING_STEPS = 2
SBA_STAGE_SKEW = 6
SBA_QT = 4096
SBA_EXIT = -104.0


def _cparams(sem):
    return pltpu.CompilerParams(dimension_semantics=sem, vmem_limit_bytes=VMEM_LIMIT)


def _sigmoid(a):
    return 1.0 / (1.0 + jnp.exp(-a))


def _silu(a):
    return a * _sigmoid(a)


def _softplus(a):
    return jnp.maximum(a, 0.0) + jnp.log(1.0 + jnp.exp(-jnp.abs(a)))


def _split_bf16(a, parts):
    out = []
    r = a
    for i in range(parts):
        p = r.astype(BF16)
        out.append(p)
        if i + 1 < parts:
            r = r - p.astype(F32)
    return out


def _rmsnorm_rows(x, w):
    ms = jnp.mean(x * x, axis=-1, keepdims=True)
    return x * lax.rsqrt(ms + EPS) * w


def _rmsnorm_kernel(x_ref, w_ref, o_ref, *, rows):
    w = w_ref[...]

    def body(r, carry):
        sl = pl.ds(pl.multiple_of(r * rows, rows), rows)
        o_ref[sl, :] = _rmsnorm_rows(x_ref[sl, :], w).astype(o_ref.dtype)
        return carry

    lax.fori_loop(0, x_ref.shape[0] // rows, body, 0)


def rmsnorm_bf16(x, w, tm=1024):
    seq, d = x.shape
    return pl.pallas_call(
        functools.partial(_rmsnorm_kernel, rows=32),
        grid=(seq // tm,),
        in_specs=[pl.BlockSpec((tm, d), lambda i: (i, 0)),
                  pl.BlockSpec((1, d), lambda i: (0, 0))],
        out_specs=pl.BlockSpec((tm, d), lambda i: (i, 0)),
        out_shape=jax.ShapeDtypeStruct((seq, d), BF16),
        compiler_params=_cparams(("arbitrary",)),
        name="rmsnorm",
    )(x, w.reshape(1, d))


NT_DIMS = (((1,), (1,)), ((), ()))
W_ROW_ALIGN = 32


def _proj_kernel(h_ref, *refs, shift, heads, scaled_tiles, scale):
    if shift:
        w_lo_ref, w_hi_ref, o_ref, wb_ref = refs
    else:
        w_lo_ref, o_ref, wb_ref = refs
    tn = wb_ref.shape[0]

    @pl.when(pl.program_id(1) == 0)
    def _():
        rows = W_ROW_ALIGN
        factor = jnp.where(pl.program_id(0) < scaled_tiles, scale, 1.0).astype(F32) if scaled_tiles else None
        stage = (lambda w: (w * factor).astype(BF16)) if scaled_tiles else (lambda w: w.astype(BF16))

        def body(r, carry):
            dst = pl.ds(pl.multiple_of(r * rows, rows), rows)
            src = pl.ds(pl.multiple_of(r * rows + shift, rows), rows)
            wb_ref[dst, :] = stage(w_lo_ref[src, :])
            return carry

        lax.fori_loop(0, (tn - shift) // rows, body, 0)
        if shift:
            wb_ref[tn - shift:tn, :] = stage(w_hi_ref[...])

    acc = lax.dot_general(h_ref[...], wb_ref[...], NT_DIMS, preferred_element_type=F32)
    if heads:
        for hh in range(o_ref.shape[0]):
            o_ref[hh] = acc[:, hh * LANES:(hh + 1) * LANES].astype(o_ref.dtype)
    else:
        o_ref[...] = acc.astype(o_ref.dtype)


def in_proj(h, w_t, layer, row0, n, heads, name, scaled_cols=0, scale=1.0, tm=1024, tn=1024):
    seq, k = h.shape
    shift = row0 % tn
    assert shift % W_ROW_ALIGN == 0 and n % tn == 0 and w_t.shape[1] >= row0 + n and scaled_cols % tn == 0
    lo0 = row0 // tn
    in_specs = [pl.BlockSpec((tm, k), lambda j, i: (i, 0)),
                pl.BlockSpec((None, tn, k), lambda j, i: (layer, lo0 + j, 0))]
    operands = [h, w_t]
    if shift:
        assert tn % shift == 0
        per = tn // shift
        in_specs.append(pl.BlockSpec((None, shift, k), lambda j, i: (layer, (lo0 + j + 1) * per, 0)))
        operands.append(w_t)
    if heads:
        hp = tn // LANES
        out_spec = pl.BlockSpec((hp, tm, LANES), lambda j, i: (j, i, 0))
        out_shape = jax.ShapeDtypeStruct((n // LANES, seq, LANES), BF16)
    else:
        out_spec = pl.BlockSpec((tm, tn), lambda j, i: (i, j))
        out_shape = jax.ShapeDtypeStruct((seq, n), BF16)
    return pl.pallas_call(
        functools.partial(_proj_kernel, shift=shift, heads=heads, scaled_tiles=scaled_cols // tn, scale=scale),
        grid=(n // tn, seq // tm),
        in_specs=in_specs,
        out_specs=out_spec,
        out_shape=out_shape,
        scratch_shapes=[pltpu.VMEM((tn, k), BF16)],
        compiler_params=_cparams(("arbitrary", "arbitrary")),
        name=name,
    )(*operands)


def _ssd_kernel(p_ref, h0_ref, hn_ref, wdt_ref, cw_ref, cb_ref, dtb_ref, alog_ref, dsk_ref, nw_ref, o_ref,
                hbuf, xbc, state, ybuf, wdt_b, chain):
    t = SSD_T
    hist = SUBLANES
    c = pl.program_id(0)
    row = lax.broadcasted_iota(jnp.int32, (t, t), 0)
    col = lax.broadcasted_iota(jnp.int32, (t, t), 1)
    tril = row >= col
    tril_b = jnp.where(tril, 1.0, 0.0).astype(BF16)
    nxt = {}

    def chain_project(h_in):
        nxt["dt_raw"] = lax.dot_general(h_in[...], wdt_b[...], NT_DIMS, preferred_element_type=F32)

    def chain_cumsum():
        dt = _softplus(nxt["dt_raw"] + dtb_ref[...])
        da = dt * -jnp.exp(alog_ref[...])
        a_cum = None
        for part in _split_bf16(da, 3):
            term = jnp.dot(tril_b, part, preferred_element_type=F32)
            a_cum = term if a_cum is None else a_cum + term
        nxt["dt"], nxt["a_cum"] = dt, a_cum

    def chain_transpose():
        nxt["a_cum_t"] = nxt["a_cum"].T
        nxt["dt_t"] = nxt["dt"].T

    def chain_store():
        a_cum_t, dt_t = nxt["a_cum_t"], nxt["dt_t"]
        a_last_t = jnp.broadcast_to(a_cum_t[:, t - 1:t], a_cum_t.shape)
        chain[0] = nxt["a_cum"]
        chain[1] = a_cum_t
        chain[2] = dt_t
        chain[3] = jnp.exp(a_last_t - a_cum_t) * dt_t

    chain_stages = (chain_cumsum, chain_transpose, chain_store)

    @pl.when(c == 0)
    def _():
        hbuf[...] = jnp.zeros_like(hbuf)
        state[...] = jnp.zeros_like(state)
        wdt_b[0:SSD_HEADS, :] = wdt_ref[...].astype(BF16)
        wdt_b[SSD_HEADS:LANES, :] = jnp.zeros((LANES - SSD_HEADS, D_MODEL), BF16)
        chain_project(h0_ref)
        for stage in chain_stages:
            stage()

    a_cum = chain[0]
    a_cum_t = chain[1]
    dt_t = chain[2]
    w_t = chain[3]
    chunk_decay = jnp.exp(a_cum[t - 1:t, :])

    shift_by = {back: jnp.where(row - col == back, 1.0, 0.0).astype(BF16) for back in range(1, SSD_CONV)}
    top = lax.broadcasted_iota(jnp.int32, (hist, CONV_BLK), 0)

    def conv_blocks(j0, j1):
        for j in range(j0, j1):
            cs = slice(j * CONV_BLK, (j + 1) * CONV_BLK)
            ub = p_ref[:, SSD_WIDTH + j * CONV_BLK:SSD_WIDTH + (j + 1) * CONV_BLK]
            u = ub.astype(F32)
            prev = hbuf[:, cs]
            acc = cb_ref[:, cs] + cw_ref[SSD_CONV - 1:SSD_CONV, cs] * u
            for back in range(1, SSD_CONV):
                sh = jnp.dot(shift_by[back], ub, preferred_element_type=F32)
                head = jnp.where(top < back, pltpu.roll(prev, back, axis=0), sh[0:hist])
                sh = jnp.concatenate([head, sh[hist:]], axis=0)
                acc = acc + cw_ref[SSD_CONV - 1 - back:SSD_CONV - back, cs] * sh
            xbc[:, cs] = _silu(acc)
            hbuf[:, cs] = u[t - hist:t]

    conv_blocks(0, CONV_DIM // CONV_BLK)
    lane = lax.broadcasted_iota(jnp.int32, (t, LANES), 1)
    low_half = lane < SSD_HEAD_DIM

    heads_per_group = SSD_HEADS // SSD_GROUPS
    pairs_per_group = heads_per_group // 2
    n_pairs = SSD_HEADS // 2
    group_vals = {}
    prepared = {}
    issued = {}

    def pair_prepare(p):
        g = p // pairs_per_group
        if p % pairs_per_group == 0:
            b_g = xbc[:, SSD_WIDTH + g * SSD_STATE:SSD_WIDTH + (g + 1) * SSD_STATE]
            c_g = xbc[:, SSD_WIDTH + SSD_BC + g * SSD_STATE:SSD_WIDTH + SSD_BC + (g + 1) * SSD_STATE]
            c_gb = c_g.astype(BF16)
            cb = lax.dot_general(c_gb, b_g.astype(BF16), NT_DIMS, preferred_element_type=F32)
            group_vals[g] = (c_gb, cb, b_g.T)
        c_gb = group_vals[g][0]
        ps = slice(2 * p * SSD_HEAD_DIM, (2 * p + 2) * SSD_HEAD_DIM)
        x_pair = xbc[:, ps]
        x_lo = jnp.where(low_half, x_pair, 0.0).astype(BF16)
        x_hi = jnp.where(low_half, 0.0, x_pair).astype(BF16)
        off = jnp.dot(c_gb, state[:, ps].astype(BF16), preferred_element_type=F32)
        prepared[p] = (jnp.concatenate([x_lo, x_hi], axis=0), off, dsk_ref[:, ps] * x_pair)

    def pair_issue(p):
        _, cb, b_gt = group_vals[p // pairs_per_group]
        h0 = 2 * p
        x_cat, off, skip = prepared.pop(p)
        ms, wms, colvs = [], [], []
        for hh in (h0, h0 + 1):
            colv = jnp.broadcast_to(a_cum[:, hh:hh + 1], (t, t))
            rowv = a_cum_t[hh:hh + 1, :]
            decay = jnp.where(tril, jnp.exp(colv - rowv), 0.0)
            ms.append((cb * decay * dt_t[hh:hh + 1, :]).astype(BF16))
            wms.append((b_gt * w_t[hh:hh + 1, :]).astype(BF16))
            colvs.append(colv)
        yd = jnp.dot(jnp.concatenate(ms, axis=1), x_cat, preferred_element_type=F32)
        sn = jnp.dot(jnp.concatenate(wms, axis=1), x_cat, preferred_element_type=F32)
        e_pair = jnp.exp(jnp.where(low_half, colvs[0], colvs[1]))
        issued[p] = (yd, sn, e_pair, off, skip)

    def pair_finish(p):
        yd, sn, e_pair, off, skip = issued.pop(p)
        h0 = 2 * p
        ps = slice(h0 * SSD_HEAD_DIM, (h0 + 2) * SSD_HEAD_DIM)
        ybuf[:, ps] = yd + e_pair * off + skip
        cd = jnp.where(low_half[0:1, :], chunk_decay[:, h0:h0 + 1], chunk_decay[:, h0 + 1:h0 + 2])
        state[:, ps] = state[:, ps] * cd + sn

    def gate_group(g):
        gwid = SSD_WIDTH // SSD_GROUPS
        rows = 32
        gs = slice(g * gwid, (g + 1) * gwid)
        for r in range(t // rows):
            rs = slice(r * rows, (r + 1) * rows)
            yg = ybuf[rs, gs] * _silu(p_ref[rs, gs].astype(F32))
            ms = jnp.mean(yg * yg, axis=-1, keepdims=True)
            o_ref[rs, gs] = (yg * lax.rsqrt(ms + EPS) * nw_ref[:, gs]).astype(o_ref.dtype)

    for p in range(n_pairs):
        pair_prepare(p)
    chain_ticks = {SSD_CHAIN_TICKS[0]: lambda: chain_project(hn_ref)}
    chain_ticks.update(zip(SSD_CHAIN_TICKS[1:], chain_stages))
    for tick in range(n_pairs + SSD_PAIR_SKEW):
        if tick in chain_ticks:
            chain_ticks[tick]()
        if tick < n_pairs:
            pair_issue(tick)
        done = tick - SSD_PAIR_SKEW
        if done >= 0:
            pair_finish(done)
            if done % pairs_per_group == pairs_per_group - 1:
                gate_group(done // pairs_per_group)


def ssd_group(p_ssd, h, layer, w_t, conv_w, conv_b, dtb, alog, dsk, norm_w):
    seq = p_ssd.shape[0]
    t = SSD_T
    assert SSD_COLS % SSD_HEADS == 0
    per_layer = lambda *shape: pl.BlockSpec((None,) + shape, lambda i: (layer,) + (0,) * len(shape))
    return pl.pallas_call(
        _ssd_kernel,
        grid=(seq // t,),
        in_specs=[pl.BlockSpec((t, SSD_COLS), lambda i: (i, 0)),
                  pl.BlockSpec((t, D_MODEL), lambda i: (0, 0)),
                  pl.BlockSpec((t, D_MODEL), lambda i: (jnp.minimum(i + 1, seq // t - 1), 0)),
                  pl.BlockSpec((None, SSD_HEADS, D_MODEL), lambda i: (layer, SSD_COLS // SSD_HEADS, 0)),
                  per_layer(SSD_CONV, CONV_DIM),
                  per_layer(1, CONV_DIM),
                  per_layer(1, LANES),
                  per_layer(1, LANES),
                  per_layer(1, SSD_WIDTH),
                  per_layer(1, SSD_WIDTH)],
        out_specs=pl.BlockSpec((t, SSD_WIDTH), lambda i: (i, 0)),
        out_shape=jax.ShapeDtypeStruct((seq, SSD_WIDTH), BF16),
        scratch_shapes=[pltpu.VMEM((SUBLANES, CONV_DIM), F32),
                        pltpu.VMEM((t, CONV_DIM), F32),
                        pltpu.VMEM((SSD_STATE, SSD_WIDTH), F32),
                        pltpu.VMEM((t, SSD_WIDTH), F32),
                        pltpu.VMEM((LANES, D_MODEL), BF16),
                        pltpu.VMEM((4, t, LANES), F32)],
        compiler_params=_cparams(("arbitrary",)),
        name="ssd_group",
    )(p_ssd, h, h, w_t, conv_w, conv_b, dtb, alog, dsk, norm_w)


def _sba_kernel(q_ref, k_ref, v_ref, g_ref, o_ref, c_ref, acc_ref):
    rows = SBA_ROWS
    kw = SBA_KEYS
    qt = q_ref.shape[1]
    n_sub = qt // rows
    subs = range(n_sub)
    qi = pl.program_id(1)
    row = lax.broadcasted_iota(jnp.int32, (rows, kw), 0)
    col = lax.broadcasted_iota(jnp.int32, (rows, kw), 1)
    col_minus_row = col - row
    r2 = lax.broadcasted_iota(jnp.int32, (SBA_SUM_PARTS * kw, 2 * kw), 0) % kw
    c2 = lax.broadcasted_iota(jnp.int32, (SBA_SUM_PARTS * kw, 2 * kw), 1)
    neg_cum = jnp.where((c2 >= kw) | (r2 > c2), -1.0, 0.0).astype(BF16)

    def sweep(ds, opening):
        grp = SBA_CUM_GROUP
        units = [(i, g0) for g0 in range(0, n_sub, grp) for i in range(len(ds))]
        last_i = len(ds) - 1
        zs, masks, sps, sums, ws_b, kss, starts = {}, {}, {}, {}, {}, {}, {}
        carries = {}
        wants = []

        def stage_scores(i, g0):
            for s in range(g0, g0 + grp):
                q0 = qi * qt + s * rows
                start = q0 + rows - (ds[i] + 1) * kw
                load = jnp.maximum(start, 0)
                starts[i, s] = start
                kss[i, s] = pl.ds(pl.multiple_of(load, rows), kw)
                zs[i, s] = lax.dot_general(q_ref[0, s * rows:(s + 1) * rows, :], k_ref[0, kss[i, s], :], NT_DIMS,
                                           preferred_element_type=F32)
                if opening and i == 0:
                    masks[i, s] = col_minus_row < (q0 - load)
                else:
                    masks[i, s] = col < (start + kw - load)

        def stage_sums(i, g0):
            parts = []
            for s in range(g0, g0 + grp):
                sps[i, s] = jnp.where(masks[i, s], _softplus(zs[i, s]), 0.0)
                parts.append(jnp.concatenate(_split_bf16(sps[i, s], SBA_SUM_PARTS), axis=1))
            both = jnp.dot(jnp.concatenate(parts, axis=0), neg_cum, preferred_element_type=F32)
            for j, s in enumerate(range(g0, g0 + grp)):
                sums[i, s] = both[j * rows:(j + 1) * rows]

        def stage_weights(i, g0):
            for s in range(g0, g0 + grp):
                if i == 0:
                    carries[s] = None if opening else c_ref[s]
                logw = zs[i, s] - sps[i, s] + sums[i, s][:, :kw]
                if carries[s] is not None:
                    logw = logw + carries[s]
                ws_b[i, s] = jnp.where(masks[i, s], jnp.exp(logw), 0.0).astype(BF16)
                tot = sums[i, s][:, kw:]
                carries[s] = tot if carries[s] is None else carries[s] + tot
                if i == last_i:
                    c_ref[s] = carries[s]
                    wants.append(jnp.where(starts[i, s] > 0, carries[s], -jnp.inf))

        def stage_values(i, g0):
            for s in range(g0, g0 + grp):
                pv = jnp.dot(ws_b[i, s], v_ref[0, kss[i, s], :], preferred_element_type=F32)
                if opening and i == 0:
                    acc_ref[s] = pv
                else:
                    acc_ref[s] += pv

        stages = (stage_scores, stage_sums, stage_weights, stage_values)
        skew = SBA_STAGE_SKEW
        for tick in range(len(units) + (len(stages) - 1) * skew):
            for j in reversed(range(len(stages))):
                u = tick - j * skew
                if 0 <= u < len(units):
                    stages[j](*units[u])
        pending = wants[0]
        for want in wants[1:]:
            pending = jnp.maximum(pending, want)
        return jnp.max(pending)

    n_open = SBA_OPENING_STEPS
    first = sweep(list(range(n_open)), True)
    lax.while_loop(lambda st: st[1] > SBA_EXIT,
                   lambda st: (st[0] + 1, sweep([st[0]], False)),
                   (jnp.int32(n_open), first))
    for s in subs:
        qs = slice(s * rows, (s + 1) * rows)
        o_ref[qs, :] = (acc_ref[s] * _silu(g_ref[0, qs, :].astype(F32))).astype(o_ref.dtype)


def sba_group(p_sba):
    seq = p_sba.shape[1]
    qt = min(SBA_QT, seq)
    nh = SBA_HEADS
    return pl.pallas_call(
        _sba_kernel,
        grid=(nh, seq // qt),
        in_specs=[pl.BlockSpec((1, qt, SBA_DIM), lambda h, i: (h, i, 0)),
                  pl.BlockSpec((1, seq, SBA_DIM), lambda h, i: (nh + h, 0, 0)),
                  pl.BlockSpec((1, seq, SBA_DIM), lambda h, i: (2 * nh + h, 0, 0)),
                  pl.BlockSpec((1, qt, SBA_DIM), lambda h, i: (3 * nh + h, i, 0))],
        out_specs=pl.BlockSpec((qt, SBA_DIM), lambda h, i: (i, h)),
        out_shape=jax.ShapeDtypeStruct((seq, SBA_WIDTH), BF16),
        scratch_shapes=[pltpu.VMEM((qt // SBA_ROWS, SBA_ROWS, SBA_KEYS), F32),
                        pltpu.VMEM((qt // SBA_ROWS, SBA_ROWS, SBA_DIM), F32)],
        compiler_params=_cparams(("arbitrary", "arbitrary")),
        name="sba_group",
    )(p_sba, p_sba, p_sba, p_sba)


def _out_proj_kernel(ya_ref, yb_ref, wa_ref, wb_ref, x_ref, nw_ref, *out_refs, emit_x):
    acc = jnp.dot(ya_ref[...], wa_ref[...], preferred_element_type=F32)
    acc = acc + jnp.dot(yb_ref[...], wb_ref[...], preferred_element_type=F32)
    x_new = x_ref[...] + acc
    if emit_x:
        xo_ref, h_ref = out_refs
        xo_ref[...] = x_new
    else:
        (h_ref,) = out_refs
    h_ref[...] = _rmsnorm_rows(x_new, nw_ref[...]).astype(h_ref.dtype)


def out_proj(y_ssd, y_sba, w_all, layer, x, norm_all, norm_idx, last, tm=256):
    seq, d = x.shape
    ka, kb = y_ssd.shape[1], y_sba.shape[1]
    assert ka == kb and w_all.shape[1] == ka + kb
    row_spec = lambda width: pl.BlockSpec((tm, width), lambda i: (i, 0))
    if last:
        out_specs = row_spec(d)
        out_shape = jax.ShapeDtypeStruct((seq, d), F32)
    else:
        out_specs = [row_spec(d), row_spec(d)]
        out_shape = [jax.ShapeDtypeStruct((seq, d), F32), jax.ShapeDtypeStruct((seq, d), BF16)]
    return pl.pallas_call(
        functools.partial(_out_proj_kernel, emit_x=not last),
        grid=(seq // tm,),
        in_specs=[row_spec(ka), row_spec(kb),
                  pl.BlockSpec((None, ka, d), lambda i: (layer, 0, 0)),
                  pl.BlockSpec((None, kb, d), lambda i: (layer, 1, 0)),
                  row_spec(d),
                  pl.BlockSpec((None, 1, d), lambda i: (norm_idx, 0, 0))],
        out_specs=out_specs,
        out_shape=out_shape,
        compiler_params=_cparams(("arbitrary",)),
        name="out_proj",
    )(y_ssd, y_sba, w_all, w_all, x, norm_all)


def kernel(x, norm_w, w_in, conv_w, conv_b, dt_bias, a_log, d_skip, ssd_norm_w, w_out, final_norm_w):
    bsz, seq, d = x.shape
    assert bsz == 1 and d == D_MODEL
    depth = w_in.shape[0]
    sba_row0 = SSD_COLS + SSD_HEADS
    pad = LANES - SSD_HEADS
    w_in_t = jnp.swapaxes(w_in, 1, 2)
    w_out_b = w_out.astype(BF16)
    dtb = jnp.pad(dt_bias, ((0, 0), (0, pad))).reshape(depth, 1, LANES)
    alog = jnp.pad(a_log, ((0, 0), (0, pad))).reshape(depth, 1, LANES)
    dsk = jnp.repeat(d_skip, SSD_HEAD_DIM, axis=1).reshape(depth, 1, SSD_WIDTH)
    conv_b3 = conv_b.reshape(depth, 1, CONV_DIM)
    ssd_nw = ssd_norm_w.reshape(depth, 1, SSD_WIDTH)
    norms = jnp.concatenate([norm_w, final_norm_w[None]], axis=0).reshape(depth + 1, 1, d)

    xs = x.reshape(seq, d)
    h = rmsnorm_bf16(xs, norm_w[0])
    for i in range(depth):
        p_ssd = in_proj(h, w_in_t, i, 0, SSD_COLS, False, "in_proj_ssd")
        p_sba = in_proj(h, w_in_t, i, sba_row0, 4 * SBA_WIDTH, True, "in_proj_sba",
                        scaled_cols=SBA_WIDTH, scale=1.0 / math.sqrt(SBA_DIM))
        y_ssd = ssd_group(p_ssd, h, i, w_in_t, conv_w, conv_b3, dtb, alog, dsk, ssd_nw)
        y_sba = sba_group(p_sba)
        last = i == depth - 1
        res = out_proj(y_ssd, y_sba, w_out_b, i, xs, norms, i + 1, last)
        if last:
            out = res
        else:
            xs, h = res
    return out.reshape(bsz, seq, d)
```

```python
import functools
import math

import jax
import jax.numpy as jnp
from jax import lax
from jax.experimental import pallas as pl
from jax.experimental.pallas import tpu as pltpu

F32 = jnp.float32
BF16 = jnp.bfloat16

D_MODEL = 2048
SSD_WIDTH = 2048
SSD_HEAD_DIM = 64
SSD_HEADS = SSD_WIDTH // SSD_HEAD_DIM
SSD_GROUPS = 4
SSD_STATE = 128
SSD_CONV = 4
SSD_BC = SSD_GROUPS * SSD_STATE
CONV_DIM = SSD_WIDTH + 2 * SSD_BC
SSD_COLS = SSD_WIDTH + CONV_DIM
SBA_HEADS = 16
SBA_DIM = 128
SBA_WIDTH = SBA_HEADS * SBA_DIM
EPS = 1e-6

LANES = 128
SUBLANES = 8
VMEM_LIMIT = 56 * 1024 * 1024

SSD_T = 128
CONV_BLK = 256
SSD_PAIR_SKEW = 2
SSD_CHAIN_TICKS = (1, 5, 10, 16)
SBA_ROWS = 64
SBA_KEYS = 128
SBA_CUM_GROUP = 2
SBA_SUM_PARTS = 1
SBA_OPENING_STEPS = 2
SBA_STAGE_SKEW = 6
SBA_QT = 4096
SBA_EXIT = -104.0


def _cparams(sem):
    return pltpu.CompilerParams(dimension_semantics=sem, vmem_limit_bytes=VMEM_LIMIT)


def _silu(a):
    half = 0.5 * a
    return half * jnp.tanh(half) + half


def _softplus(a):
    return jnp.maximum(a, 0.0) + jnp.log(1.0 + jnp.exp(-jnp.abs(a)))


def _split_bf16(a, parts):
    out = []
    r = a
    for i in range(parts):
        p = r.astype(BF16)
        out.append(p)
        if i + 1 < parts:
            r = r - p.astype(F32)
    return out


def _rmsnorm_rows(x, w):
    ms = jnp.mean(x * x, axis=-1, keepdims=True)
    return x * lax.rsqrt(ms + EPS) * w


def _rmsnorm_kernel(x_ref, w_ref, o_ref, *, rows):
    w = w_ref[...]

    def body(r, carry):
        sl = pl.ds(pl.multiple_of(r * rows, rows), rows)
        o_ref[sl, :] = _rmsnorm_rows(x_ref[sl, :], w).astype(o_ref.dtype)
        return carry

    lax.fori_loop(0, x_ref.shape[0] // rows, body, 0)


def rmsnorm_bf16(x, w, tm=1024):
    seq, d = x.shape
    return pl.pallas_call(
        functools.partial(_rmsnorm_kernel, rows=32),
        grid=(seq // tm,),
        in_specs=[pl.BlockSpec((tm, d), lambda i: (i, 0)),
                  pl.BlockSpec((1, d), lambda i: (0, 0))],
        out_specs=pl.BlockSpec((tm, d), lambda i: (i, 0)),
        out_shape=jax.ShapeDtypeStruct((seq, d), BF16),
        compiler_params=_cparams(("arbitrary",)),
        name="rmsnorm",
    )(x, w.reshape(1, d))


NT_DIMS = (((1,), (1,)), ((), ()))
W_ROW_ALIGN = 32


def _proj_kernel(h_ref, *refs, shift, heads, scaled_tiles, scale):
    if shift:
        w_lo_ref, w_hi_ref, o_ref, wb_ref = refs
    else:
        w_lo_ref, o_ref, wb_ref = refs
    tn = wb_ref.shape[0]

    @pl.when(pl.program_id(1) == 0)
    def _():
        rows = W_ROW_ALIGN
        factor = jnp.where(pl.program_id(0) < scaled_tiles, scale, 1.0).astype(F32) if scaled_tiles else None
        stage = (lambda w: (w * factor).astype(BF16)) if scaled_tiles else (lambda w: w.astype(BF16))

        def body(r, carry):
            dst = pl.ds(pl.multiple_of(r * rows, rows), rows)
            src = pl.ds(pl.multiple_of(r * rows + shift, rows), rows)
            wb_ref[dst, :] = stage(w_lo_ref[src, :])
            return carry

        lax.fori_loop(0, (tn - shift) // rows, body, 0)
        if shift:
            wb_ref[tn - shift:tn, :] = stage(w_hi_ref[...])

    acc = lax.dot_general(h_ref[...], wb_ref[...], NT_DIMS, preferred_element_type=F32)
    if heads:
        for hh in range(o_ref.shape[0]):
            o_ref[hh] = acc[:, hh * LANES:(hh + 1) * LANES].astype(o_ref.dtype)
    else:
        o_ref[...] = acc.astype(o_ref.dtype)


def in_proj(h, w_t, layer, row0, n, heads, name, scaled_cols=0, scale=1.0, tm=1024, tn=1024):
    seq, k = h.shape
    shift = row0 % tn
    assert shift % W_ROW_ALIGN == 0 and n % tn == 0 and w_t.shape[1] >= row0 + n and scaled_cols % tn == 0
    lo0 = row0 // tn
    in_specs = [pl.BlockSpec((tm, k), lambda j, i: (i, 0)),
                pl.BlockSpec((None, tn, k), lambda j, i: (layer, lo0 + j, 0))]
    operands = [h, w_t]
    if shift:
        assert tn % shift == 0
        per = tn // shift
        in_specs.append(pl.BlockSpec((None, shift, k), lambda j, i: (layer, (lo0 + j + 1) * per, 0)))
        operands.append(w_t)
    if heads:
        hp = tn // LANES
        out_spec = pl.BlockSpec((hp, tm, LANES), lambda j, i: (j, i, 0))
        out_shape = jax.ShapeDtypeStruct((n // LANES, seq, LANES), BF16)
    else:
        out_spec = pl.BlockSpec((tm, tn), lambda j, i: (i, j))
        out_shape = jax.ShapeDtypeStruct((seq, n), BF16)
    return pl.pallas_call(
        functools.partial(_proj_kernel, shift=shift, heads=heads, scaled_tiles=scaled_cols // tn, scale=scale),
        grid=(n // tn, seq // tm),
        in_specs=in_specs,
        out_specs=out_spec,
        out_shape=out_shape,
        scratch_shapes=[pltpu.VMEM((tn, k), BF16)],
        compiler_params=_cparams(("arbitrary", "arbitrary")),
        name=name,
    )(*operands)


def _ssd_kernel(p_ref, h0_ref, hn_ref, wdt_ref, cw_ref, cb_ref, dtb_ref, alog_ref, dsk_ref, nw_ref, o_ref,
                hbuf, xbc, state, ybuf, wdt_b, chain):
    t = SSD_T
    hist = SUBLANES
    c = pl.program_id(0)
    row = lax.broadcasted_iota(jnp.int32, (t, t), 0)
    col = lax.broadcasted_iota(jnp.int32, (t, t), 1)
    tril = row >= col
    tril_b = jnp.where(tril, 1.0, 0.0).astype(BF16)
    nxt = {}

    def chain_project(h_in):
        nxt["dt_raw"] = lax.dot_general(h_in[...], wdt_b[...], NT_DIMS, preferred_element_type=F32)

    def chain_cumsum():
        dt = _softplus(nxt["dt_raw"] + dtb_ref[...])
        da = dt * -jnp.exp(alog_ref[...])
        a_cum = None
        for part in _split_bf16(da, 3):
            term = jnp.dot(tril_b, part, preferred_element_type=F32)
            a_cum = term if a_cum is None else a_cum + term
        nxt["dt"], nxt["a_cum"] = dt, a_cum

    def chain_transpose():
        nxt["a_cum_t"] = nxt["a_cum"].T
        nxt["dt_t"] = nxt["dt"].T

    def chain_store():
        a_cum_t, dt_t = nxt["a_cum_t"], nxt["dt_t"]
        a_last_t = jnp.broadcast_to(a_cum_t[:, t - 1:t], a_cum_t.shape)
        chain[0] = nxt["a_cum"]
        chain[1] = a_cum_t
        chain[2] = dt_t
        chain[3] = jnp.exp(a_last_t - a_cum_t) * dt_t

    chain_stages = (chain_cumsum, chain_transpose, chain_store)

    @pl.when(c == 0)
    def _():
        hbuf[...] = jnp.zeros_like(hbuf)
        state[...] = jnp.zeros_like(state)
        wdt_b[0:SSD_HEADS, :] = wdt_ref[...].astype(BF16)
        wdt_b[SSD_HEADS:LANES, :] = jnp.zeros((LANES - SSD_HEADS, D_MODEL), BF16)
        chain_project(h0_ref)
        for stage in chain_stages:
            stage()

    a_cum = chain[0]
    a_cum_t = chain[1]
    dt_t = chain[2]
    w_t = chain[3]
    chunk_decay = jnp.exp(a_cum[t - 1:t, :])

    shift_by = {back: jnp.where(row - col == back, 1.0, 0.0).astype(BF16) for back in range(1, SSD_CONV)}
    top = lax.broadcasted_iota(jnp.int32, (hist, CONV_BLK), 0)

    def conv_blocks(j0, j1):
        for j in range(j0, j1):
            cs = slice(j * CONV_BLK, (j + 1) * CONV_BLK)
            ub = p_ref[:, SSD_WIDTH + j * CONV_BLK:SSD_WIDTH + (j + 1) * CONV_BLK]
            u = ub.astype(F32)
            prev = hbuf[:, cs]
            acc = cb_ref[:, cs] + cw_ref[SSD_CONV - 1:SSD_CONV, cs] * u
            for back in range(1, SSD_CONV):
                sh = jnp.dot(shift_by[back], ub, preferred_element_type=F32)
                head = jnp.where(top < back, pltpu.roll(prev, back, axis=0), sh[0:hist])
                sh = jnp.concatenate([head, sh[hist:]], axis=0)
                acc = acc + cw_ref[SSD_CONV - 1 - back:SSD_CONV - back, cs] * sh
            xbc[:, cs] = _silu(acc)
            hbuf[:, cs] = u[t - hist:t]

    conv_blocks(0, CONV_DIM // CONV_BLK)
    lane = lax.broadcasted_iota(jnp.int32, (t, LANES), 1)
    low_half = lane < SSD_HEAD_DIM

    heads_per_group = SSD_HEADS // SSD_GROUPS
    pairs_per_group = heads_per_group // 2
    n_pairs = SSD_HEADS // 2
    group_vals = {}
    prepared = {}
    issued = {}

    def pair_prepare(p):
        g = p // pairs_per_group
        if p % pairs_per_group == 0:
            b_g = xbc[:, SSD_WIDTH + g * SSD_STATE:SSD_WIDTH + (g + 1) * SSD_STATE]
            c_g = xbc[:, SSD_WIDTH + SSD_BC + g * SSD_STATE:SSD_WIDTH + SSD_BC + (g + 1) * SSD_STATE]
            c_gb = c_g.astype(BF16)
            cb = lax.dot_general(c_gb, b_g.astype(BF16), NT_DIMS, preferred_element_type=F32)
            group_vals[g] = (c_gb, cb, b_g.T)
        c_gb = group_vals[g][0]
        ps = slice(2 * p * SSD_HEAD_DIM, (2 * p + 2) * SSD_HEAD_DIM)
        x_pair = xbc[:, ps]
        x_lo = jnp.where(low_half, x_pair, 0.0).astype(BF16)
        x_hi = jnp.where(low_half, 0.0, x_pair).astype(BF16)
        off = jnp.dot(c_gb, state[:, ps].astype(BF16), preferred_element_type=F32)
        prepared[p] = (jnp.concatenate([x_lo, x_hi], axis=0), off, dsk_ref[:, ps] * x_pair)

    def pair_issue(p):
        _, cb, b_gt = group_vals[p // pairs_per_group]
        h0 = 2 * p
        x_cat, off, skip = prepared.pop(p)
        ms, wms, colvs = [], [], []
        for hh in (h0, h0 + 1):
            colv = jnp.broadcast_to(a_cum[:, hh:hh + 1], (t, t))
            rowv = a_cum_t[hh:hh + 1, :]
            decay = jnp.where(tril, jnp.exp(colv - rowv), 0.0)
            ms.append((cb * decay * dt_t[hh:hh + 1, :]).astype(BF16))
            wms.append((b_gt * w_t[hh:hh + 1, :]).astype(BF16))
            colvs.append(colv)
        yd = jnp.dot(jnp.concatenate(ms, axis=1), x_cat, preferred_element_type=F32)
        sn = jnp.dot(jnp.concatenate(wms, axis=1), x_cat, preferred_element_type=F32)
        e_pair = jnp.exp(jnp.where(low_half, colvs[0], colvs[1]))
        issued[p] = (yd, sn, e_pair, off, skip)

    def pair_finish(p):
        yd, sn, e_pair, off, skip = issued.pop(p)
        h0 = 2 * p
        ps = slice(h0 * SSD_HEAD_DIM, (h0 + 2) * SSD_HEAD_DIM)
        ybuf[:, ps] = yd + e_pair * off + skip
        cd = jnp.where(low_half[0:1, :], chunk_decay[:, h0:h0 + 1], chunk_decay[:, h0 + 1:h0 + 2])
        state[:, ps] = state[:, ps] * cd + sn

    def gate_group(g):
        gwid = SSD_WIDTH // SSD_GROUPS
        rows = 32
        gs = slice(g * gwid, (g + 1) * gwid)
        for r in range(t // rows):
            rs = slice(r * rows, (r + 1) * rows)
            yg = ybuf[rs, gs] * _silu(p_ref[rs, gs].astype(F32))
            ms = jnp.mean(yg * yg, axis=-1, keepdims=True)
            o_ref[rs, gs] = (yg * lax.rsqrt(ms + EPS) * nw_ref[:, gs]).astype(o_ref.dtype)

    for p in range(n_pairs):
        pair_prepare(p)
    chain_ticks = {SSD_CHAIN_TICKS[0]: lambda: chain_project(hn_ref)}
    chain_ticks.update(zip(SSD_CHAIN_TICKS[1:], chain_stages))
    for tick in range(n_pairs + SSD_PAIR_SKEW):
        if tick in chain_ticks:
            chain_ticks[tick]()
        if tick < n_pairs:
            pair_issue(tick)
        done = tick - SSD_PAIR_SKEW
        if done >= 0:
            pair_finish(done)
            if done % pairs_per_group == pairs_per_group - 1:
                gate_group(done // pairs_per_group)


def ssd_group(p_ssd, h, layer, w_t, conv_w, conv_b, dtb, alog, dsk, norm_w):
    seq = p_ssd.shape[0]
    t = SSD_T
    assert SSD_COLS % SSD_HEADS == 0
    per_layer = lambda *shape: pl.BlockSpec((None,) + shape, lambda i: (layer,) + (0,) * len(shape))
    return pl.pallas_call(
        _ssd_kernel,
        grid=(seq // t,),
        in_specs=[pl.BlockSpec((t, SSD_COLS), lambda i: (i, 0)),
                  pl.BlockSpec((t, D_MODEL), lambda i: (0, 0)),
                  pl.BlockSpec((t, D_MODEL), lambda i: (jnp.minimum(i + 1, seq // t - 1), 0)),
                  pl.BlockSpec((None, SSD_HEADS, D_MODEL), lambda i: (layer, SSD_COLS // SSD_HEADS, 0)),
                  per_layer(SSD_CONV, CONV_DIM),
                  per_layer(1, CONV_DIM),
                  per_layer(1, LANES),
                  per_layer(1, LANES),
                  per_layer(1, SSD_WIDTH),
                  per_layer(1, SSD_WIDTH)],
        out_specs=pl.BlockSpec((t, SSD_WIDTH), lambda i: (i, 0)),
        out_shape=jax.ShapeDtypeStruct((seq, SSD_WIDTH), BF16),
        scratch_shapes=[pltpu.VMEM((SUBLANES, CONV_DIM), F32),
                        pltpu.VMEM((t, CONV_DIM), F32),
                        pltpu.VMEM((SSD_STATE, SSD_WIDTH), F32),
                        pltpu.VMEM((t, SSD_WIDTH), F32),
                        pltpu.VMEM((LANES, D_MODEL), BF16),
                        pltpu.VMEM((4, t, LANES), F32)],
        compiler_params=_cparams(("arbitrary",)),
        name="ssd_group",
    )(p_ssd, h, h, w_t, conv_w, conv_b, dtb, alog, dsk, norm_w)


def _sba_kernel(q_ref, k_ref, v_ref, g_ref, o_ref, c_ref, acc_ref):
    rows = SBA_ROWS
    kw = SBA_KEYS
    qt = q_ref.shape[1]
    n_sub = qt // rows
    subs = range(n_sub)
    qi = pl.program_id(1)
    row = lax.broadcasted_iota(jnp.int32, (rows, kw), 0)
    col = lax.broadcasted_iota(jnp.int32, (rows, kw), 1)
    col_minus_row = col - row
    r2 = lax.broadcasted_iota(jnp.int32, (SBA_SUM_PARTS * kw, 2 * kw), 0) % kw
    c2 = lax.broadcasted_iota(jnp.int32, (SBA_SUM_PARTS * kw, 2 * kw), 1)
    neg_cum = jnp.where((c2 >= kw) | (r2 > c2), -1.0, 0.0).astype(BF16)

    def sweep(ds, opening):
        grp = SBA_CUM_GROUP
        units = [(i, g0) for g0 in range(0, n_sub, grp) for i in range(len(ds))]
        last_i = len(ds) - 1
        zs, masks, sps, sums, ws_b, kss, starts = {}, {}, {}, {}, {}, {}, {}
        carries = {}
        wants = []

        def stage_scores(i, g0):
            for s in range(g0, g0 + grp):
                q0 = qi * qt + s * rows
                start = q0 + rows - (ds[i] + 1) * kw
                load = jnp.maximum(start, 0)
                starts[i, s] = start
                kss[i, s] = pl.ds(pl.multiple_of(load, rows), kw)
                zs[i, s] = lax.dot_general(q_ref[0, s * rows:(s + 1) * rows, :], k_ref[0, kss[i, s], :], NT_DIMS,
                                           preferred_element_type=F32)
                if opening and i == 0:
                    masks[i, s] = col_minus_row < (q0 - load)
                else:
                    masks[i, s] = col < (start + kw - load)

        def stage_sums(i, g0):
            parts = []
            for s in range(g0, g0 + grp):
                sps[i, s] = jnp.where(masks[i, s], _softplus(zs[i, s]), 0.0)
                parts.append(jnp.concatenate(_split_bf16(sps[i, s], SBA_SUM_PARTS), axis=1))
            both = jnp.dot(jnp.concatenate(parts, axis=0), neg_cum, preferred_element_type=F32)
            for j, s in enumerate(range(g0, g0 + grp)):
                sums[i, s] = both[j * rows:(j + 1) * rows]

        def stage_weights(i, g0):
            for s in range(g0, g0 + grp):
                if i == 0:
                    carries[s] = None if opening else c_ref[s]
                logw = zs[i, s] - sps[i, s] + sums[i, s][:, :kw]
                if carries[s] is not None:
                    logw = logw + carries[s]
                ws_b[i, s] = jnp.where(masks[i, s], jnp.exp(logw), 0.0).astype(BF16)
                tot = sums[i, s][:, kw:]
                carries[s] = tot if carries[s] is None else carries[s] + tot
                if i == last_i:
                    c_ref[s] = carries[s]
                    wants.append(jnp.where(starts[i, s] > 0, carries[s], -jnp.inf))

        def stage_values(i, g0):
            for s in range(g0, g0 + grp):
                pv = jnp.dot(ws_b[i, s], v_ref[0, kss[i, s], :], preferred_element_type=F32)
                if opening and i == 0:
                    acc_ref[s] = pv
                else:
                    acc_ref[s] += pv

        stages = (stage_scores, stage_sums, stage_weights, stage_values)
        skew = SBA_STAGE_SKEW
        for tick in range(len(units) + (len(stages) - 1) * skew):
            for j in reversed(range(len(stages))):
                u = tick - j * skew
                if 0 <= u < len(units):
                    stages[j](*units[u])
        pending = wants[0]
        for want in wants[1:]:
            pending = jnp.maximum(pending, want)
        return jnp.max(pending)

    n_open = SBA_OPENING_STEPS
    first = sweep(list(range(n_open)), True)
    lax.while_loop(lambda st: st[1] > SBA_EXIT,
                   lambda st: (st[0] + 1, sweep([st[0]], False)),
                   (jnp.int32(n_open), first))
    for s in subs:
        qs = slice(s * rows, (s + 1) * rows)
        o_ref[qs, :] = (acc_ref[s] * _silu(g_ref[0, qs, :].astype(F32))).astype(o_ref.dtype)


def sba_group(p_sba):
    seq = p_sba.shape[1]
    qt = min(SBA_QT, seq)
    nh = SBA_HEADS
    return pl.pallas_call(
        _sba_kernel,
        grid=(nh, seq // qt),
        in_specs=[pl.BlockSpec((1, qt, SBA_DIM), lambda h, i: (h, i, 0)),
                  pl.BlockSpec((1, seq, SBA_DIM), lambda h, i: (nh + h, 0, 0)),
                  pl.BlockSpec((1, seq, SBA_DIM), lambda h, i: (2 * nh + h, 0, 0)),
                  pl.BlockSpec((1, qt, SBA_DIM), lambda h, i: (3 * nh + h, i, 0))],
        out_specs=pl.BlockSpec((qt, SBA_DIM), lambda h, i: (i, h)),
        out_shape=jax.ShapeDtypeStruct((seq, SBA_WIDTH), BF16),
        scratch_shapes=[pltpu.VMEM((qt // SBA_ROWS, SBA_ROWS, SBA_KEYS), F32),
                        pltpu.VMEM((qt // SBA_ROWS, SBA_ROWS, SBA_DIM), F32)],
        compiler_params=_cparams(("arbitrary", "arbitrary")),
        name="sba_group",
    )(p_sba, p_sba, p_sba, p_sba)


def _out_proj_kernel(ya_ref, yb_ref, wa_ref, wb_ref, x_ref, nw_ref, *out_refs, emit_x):
    acc = jnp.dot(ya_ref[...], wa_ref[...], preferred_element_type=F32)
    acc = acc + jnp.dot(yb_ref[...], wb_ref[...], preferred_element_type=F32)
    x_new = x_ref[...] + acc
    if emit_x:
        xo_ref, h_ref = out_refs
        xo_ref[...] = x_new
    else:
        (h_ref,) = out_refs
    h_ref[...] = _rmsnorm_rows(x_new, nw_ref[...]).astype(h_ref.dtype)


def out_proj(y_ssd, y_sba, w_all, layer, x, norm_all, norm_idx, last, tm=256):
    seq, d = x.shape
    ka, kb = y_ssd.shape[1], y_sba.shape[1]
    assert ka == kb and w_all.shape[1] == ka + kb
    row_spec = lambda width: pl.BlockSpec((tm, width), lambda i: (i, 0))
    if last:
        out_specs = row_spec(d)
        out_shape = jax.ShapeDtypeStruct((seq, d), F32)
    else:
        out_specs = [row_spec(d), row_spec(d)]
        out_shape = [jax.ShapeDtypeStruct((seq, d), F32), jax.ShapeDtypeStruct((seq, d), BF16)]
    return pl.pallas_call(
        functools.partial(_out_proj_kernel, emit_x=not last),
        grid=(seq // tm,),
        in_specs=[row_spec(ka), row_spec(kb),
                  pl.BlockSpec((None, ka, d), lambda i: (layer, 0, 0)),
                  pl.BlockSpec((None, kb, d), lambda i: (layer, 1, 0)),
                  row_spec(d),
                  pl.BlockSpec((None, 1, d), lambda i: (norm_idx, 0, 0))],
        out_specs=out_specs,
        out_shape=out_shape,
        compiler_params=_cparams(("arbitrary",)),
        name="out_proj",
    )(y_ssd, y_sba, w_all, w_all, x, norm_all)


def kernel(x, norm_w, w_in, conv_w, conv_b, dt_bias, a_log, d_skip, ssd_norm_w, w_out, final_norm_w):
    bsz, seq, d = x.shape
    assert bsz == 1 and d == D_MODEL
    depth = w_in.shape[0]
    sba_row0 = SSD_COLS + SSD_HEADS
    pad = LANES - SSD_HEADS
    w_in_t = jnp.swapaxes(w_in, 1, 2)
    w_out_b = w_out.astype(BF16)
    dtb = jnp.pad(dt_bias, ((0, 0), (0, pad))).reshape(depth, 1, LANES)
    alog = jnp.pad(a_log, ((0, 0), (0, pad))).reshape(depth, 1, LANES)
    dsk = jnp.repeat(d_skip, SSD_HEAD_DIM, axis=1).reshape(depth, 1, SSD_WIDTH)
    conv_b3 = conv_b.reshape(depth, 1, CONV_DIM)
    ssd_nw = ssd_norm_w.reshape(depth, 1, SSD_WIDTH)
    norms = jnp.concatenate([norm_w, final_norm_w[None]], axis=0).reshape(depth + 1, 1, d)

    xs = x.reshape(seq, d)
    h = rmsnorm_bf16(xs, norm_w[0])
    for i in range(depth):
        p_ssd = in_proj(h, w_in_t, i, 0, SSD_COLS, False, "in_proj_ssd")
        p_sba = in_proj(h, w_in_t, i, sba_row0, 4 * SBA_WIDTH, True, "in_proj_sba",
                        scaled_cols=SBA_WIDTH, scale=1.0 / math.sqrt(SBA_DIM))
        y_ssd = ssd_group(p_ssd, h, i, w_in_t, conv_w, conv_b3, dtb, alog, dsk, ssd_nw)
        y_sba = sba_group(p_sba)
        last = i == depth - 1
        res = out_proj(y_ssd, y_sba, w_out_b, i, xs, norms, i + 1, last)
        if last:
            out = res
        else:
            xs, h = res
    return out.reshape(bsz, seq, d)
```

```python
import functools
import math

import jax
import jax.numpy as jnp
from jax import lax
from jax.experimental import pallas as pl
from jax.experimental.pallas import tpu as pltpu

F32 = jnp.float32
BF16 = jnp.bfloat16

D_MODEL = 2048
SSD_WIDTH = 2048
SSD_HEAD_DIM = 64
SSD_HEADS = SSD_WIDTH // SSD_HEAD_DIM
SSD_GROUPS = 4
SSD_STATE = 128
SSD_CONV = 4
SSD_BC = SSD_GROUPS * SSD_STATE
CONV_DIM = SSD_WIDTH + 2 * SSD_BC
SSD_COLS = SSD_WIDTH + CONV_DIM
SBA_HEADS = 16
SBA_DIM = 128
SBA_WIDTH = SBA_HEADS * SBA_DIM
EPS = 1e-6

LANES = 128
SUBLANES = 8
VMEM_LIMIT = 56 * 1024 * 1024

SSD_T = 128
CONV_BLK = 256
SSD_PAIR_SKEW = 4
SSD_CHAIN_TICKS = (1, 5, 10, 16)
SBA_ROWS = 64
SBA_KEYS = 128
SBA_CUM_GROUP = 2
SBA_SUM_PARTS = 1
SBA_OPENING_STEPS = 2
SBA_STAGE_SKEW = 4
SBA_QT = 8192
SBA_EXIT = -104.0


def _cparams(sem):
    return pltpu.CompilerParams(dimension_semantics=sem, vmem_limit_bytes=VMEM_LIMIT)


def _silu(a):
    half = 0.5 * a
    return half * jnp.tanh(half) + half


def _softplus(a):
    return jnp.maximum(a, 0.0) + jnp.log(1.0 + jnp.exp(-jnp.abs(a)))


def _split_bf16(a, parts):
    out = []
    r = a
    for i in range(parts):
        p = r.astype(BF16)
        out.append(p)
        if i + 1 < parts:
            r = r - p.astype(F32)
    return out


def _rmsnorm_rows(x, w):
    ms = jnp.mean(x * x, axis=-1, keepdims=True)
    return x * lax.rsqrt(ms + EPS) * w


def _rmsnorm_kernel(x_ref, w_ref, o_ref, *, rows):
    w = w_ref[...]

    def body(r, carry):
        sl = pl.ds(pl.multiple_of(r * rows, rows), rows)
        o_ref[sl, :] = _rmsnorm_rows(x_ref[sl, :], w).astype(o_ref.dtype)
        return carry

    lax.fori_loop(0, x_ref.shape[0] // rows, body, 0)


def rmsnorm_bf16(x, w, tm=1024):
    seq, d = x.shape
    return pl.pallas_call(
        functools.partial(_rmsnorm_kernel, rows=32),
        grid=(seq // tm,),
        in_specs=[pl.BlockSpec((tm, d), lambda i: (i, 0)),
                  pl.BlockSpec((1, d), lambda i: (0, 0))],
        out_specs=pl.BlockSpec((tm, d), lambda i: (i, 0)),
        out_shape=jax.ShapeDtypeStruct((seq, d), BF16),
        compiler_params=_cparams(("arbitrary",)),
        name="rmsnorm",
    )(x, w.reshape(1, d))


NT_DIMS = (((1,), (1,)), ((), ()))
W_ROW_ALIGN = 32


def _proj_kernel(h_ref, *refs, shift, heads, scaled_tiles, scale):
    if shift:
        w_lo_ref, w_hi_ref, o_ref, wb_ref = refs
    else:
        w_lo_ref, o_ref, wb_ref = refs
    tn = wb_ref.shape[0]

    @pl.when(pl.program_id(1) == 0)
    def _():
        rows = W_ROW_ALIGN
        factor = jnp.where(pl.program_id(0) < scaled_tiles, scale, 1.0).astype(F32) if scaled_tiles else None
        stage = (lambda w: (w * factor).astype(BF16)) if scaled_tiles else (lambda w: w.astype(BF16))

        def body(r, carry):
            dst = pl.ds(pl.multiple_of(r * rows, rows), rows)
            src = pl.ds(pl.multiple_of(r * rows + shift, rows), rows)
            wb_ref[dst, :] = stage(w_lo_ref[src, :])
            return carry

        lax.fori_loop(0, (tn - shift) // rows, body, 0)
        if shift:
            wb_ref[tn - shift:tn, :] = stage(w_hi_ref[...])

    acc = lax.dot_general(h_ref[...], wb_ref[...], NT_DIMS, preferred_element_type=F32)
    if heads:
        for hh in range(o_ref.shape[0]):
            o_ref[hh] = acc[:, hh * LANES:(hh + 1) * LANES].astype(o_ref.dtype)
    else:
        o_ref[...] = acc.astype(o_ref.dtype)


def in_proj(h, w_t, layer, row0, n, heads, name, scaled_cols=0, scale=1.0, tm=1024, tn=1024):
    seq, k = h.shape
    shift = row0 % tn
    assert shift % W_ROW_ALIGN == 0 and n % tn == 0 and w_t.shape[1] >= row0 + n and scaled_cols % tn == 0
    lo0 = row0 // tn
    in_specs = [pl.BlockSpec((tm, k), lambda j, i: (i, 0)),
                pl.BlockSpec((None, tn, k), lambda j, i: (layer, lo0 + j, 0))]
    operands = [h, w_t]
    if shift:
        assert tn % shift == 0
        per = tn // shift
        in_specs.append(pl.BlockSpec((None, shift, k), lambda j, i: (layer, (lo0 + j + 1) * per, 0)))
        operands.append(w_t)
    if heads:
        hp = tn // LANES
        out_spec = pl.BlockSpec((hp, tm, LANES), lambda j, i: (j, i, 0))
        out_shape = jax.ShapeDtypeStruct((n // LANES, seq, LANES), BF16)
    else:
        out_spec = pl.BlockSpec((tm, tn), lambda j, i: (i, j))
        out_shape = jax.ShapeDtypeStruct((seq, n), BF16)
    return pl.pallas_call(
        functools.partial(_proj_kernel, shift=shift, heads=heads, scaled_tiles=scaled_cols // tn, scale=scale),
        grid=(n // tn, seq // tm),
        in_specs=in_specs,
        out_specs=out_spec,
        out_shape=out_shape,
        scratch_shapes=[pltpu.VMEM((tn, k), BF16)],
        compiler_params=_cparams(("arbitrary", "arbitrary")),
        name=name,
    )(*operands)


def _ssd_kernel(p_ref, h0_ref, hn_ref, wdt_ref, cw_ref, cb_ref, dtb_ref, alog_ref, dsk_ref, nw_ref, o_ref,
                hbuf, xbc, state, ybuf, wdt_b, chain):
    t = SSD_T
    hist = SUBLANES
    c = pl.program_id(0)
    row = lax.broadcasted_iota(jnp.int32, (t, t), 0)
    col = lax.broadcasted_iota(jnp.int32, (t, t), 1)
    tril = row >= col
    tril_b = jnp.where(tril, 1.0, 0.0).astype(BF16)
    nxt = {}

    def chain_project(h_in):
        nxt["dt_raw"] = lax.dot_general(h_in[...], wdt_b[...], NT_DIMS, preferred_element_type=F32)

    def chain_cumsum():
        dt = _softplus(nxt["dt_raw"] + dtb_ref[...])
        da = dt * -jnp.exp(alog_ref[...])
        a_cum = None
        for part in _split_bf16(da, 3):
            term = jnp.dot(tril_b, part, preferred_element_type=F32)
            a_cum = term if a_cum is None else a_cum + term
        nxt["dt"], nxt["a_cum"] = dt, a_cum

    def chain_transpose():
        nxt["a_cum_t"] = nxt["a_cum"].T
        nxt["dt_t"] = nxt["dt"].T

    def chain_store():
        a_cum_t, dt_t = nxt["a_cum_t"], nxt["dt_t"]
        a_last_t = jnp.broadcast_to(a_cum_t[:, t - 1:t], a_cum_t.shape)
        chain[0] = nxt["a_cum"]
        chain[1] = a_cum_t
        chain[2] = dt_t
        chain[3] = jnp.exp(a_last_t - a_cum_t) * dt_t

    chain_stages = (chain_cumsum, chain_transpose, chain_store)

    @pl.when(c == 0)
    def _():
        hbuf[...] = jnp.zeros_like(hbuf)
        state[...] = jnp.zeros_like(state)
        wdt_b[0:SSD_HEADS, :] = wdt_ref[...].astype(BF16)
        wdt_b[SSD_HEADS:LANES, :] = jnp.zeros((LANES - SSD_HEADS, D_MODEL), BF16)
        chain_project(h0_ref)
        for stage in chain_stages:
            stage()

    a_cum = chain[0]
    a_cum_t = chain[1]
    dt_t = chain[2]
    w_t = chain[3]
    chunk_decay = jnp.exp(a_cum[t - 1:t, :])

    shift_by = {back: jnp.where(row - col == back, 1.0, 0.0).astype(BF16) for back in range(1, SSD_CONV)}
    top = lax.broadcasted_iota(jnp.int32, (hist, CONV_BLK), 0)

    def conv_blocks(j0, j1):
        for j in range(j0, j1):
            cs = slice(j * CONV_BLK, (j + 1) * CONV_BLK)
            ub = p_ref[:, SSD_WIDTH + j * CONV_BLK:SSD_WIDTH + (j + 1) * CONV_BLK]
            u = ub.astype(F32)
            prev = hbuf[:, cs]
            acc = cb_ref[:, cs] + cw_ref[SSD_CONV - 1:SSD_CONV, cs] * u
            for back in range(1, SSD_CONV):
                sh = jnp.dot(shift_by[back], ub, preferred_element_type=F32)
                head = jnp.where(top < back, pltpu.roll(prev, back, axis=0), sh[0:hist])
                sh = jnp.concatenate([head, sh[hist:]], axis=0)
                acc = acc + cw_ref[SSD_CONV - 1 - back:SSD_CONV - back, cs] * sh
            xbc[:, cs] = _silu(acc)
            hbuf[:, cs] = u[t - hist:t]

    conv_blocks(0, CONV_DIM // CONV_BLK)
    lane = lax.broadcasted_iota(jnp.int32, (t, LANES), 1)
    low_half = lane < SSD_HEAD_DIM

    heads_per_group = SSD_HEADS // SSD_GROUPS
    pairs_per_group = heads_per_group // 2
    n_pairs = SSD_HEADS // 2
    group_vals = {}
    prepared = {}
    issued = {}

    def pair_prepare(p):
        g = p // pairs_per_group
        if p % pairs_per_group == 0:
            b_g = xbc[:, SSD_WIDTH + g * SSD_STATE:SSD_WIDTH + (g + 1) * SSD_STATE]
            c_g = xbc[:, SSD_WIDTH + SSD_BC + g * SSD_STATE:SSD_WIDTH + SSD_BC + (g + 1) * SSD_STATE]
            c_gb = c_g.astype(BF16)
            cb = lax.dot_general(c_gb, b_g.astype(BF16), NT_DIMS, preferred_element_type=F32)
            group_vals[g] = (c_gb, cb, b_g.T)
        c_gb = group_vals[g][0]
        ps = slice(2 * p * SSD_HEAD_DIM, (2 * p + 2) * SSD_HEAD_DIM)
        x_pair = xbc[:, ps]
        x_lo = jnp.where(low_half, x_pair, 0.0).astype(BF16)
        x_hi = jnp.where(low_half, 0.0, x_pair).astype(BF16)
        off = jnp.dot(c_gb, state[:, ps].astype(BF16), preferred_element_type=F32)
        prepared[p] = (jnp.concatenate([x_lo, x_hi], axis=0), off, dsk_ref[:, ps] * x_pair)

    def pair_issue(p):
        _, cb, b_gt = group_vals[p // pairs_per_group]
        h0 = 2 * p
        x_cat, off, skip = prepared.pop(p)
        ms, wms, colvs = [], [], []
        for hh in (h0, h0 + 1):
            colv = jnp.broadcast_to(a_cum[:, hh:hh + 1], (t, t))
            rowv = a_cum_t[hh:hh + 1, :]
            decay = jnp.where(tril, jnp.exp(colv - rowv), 0.0)
            ms.append((cb * decay * dt_t[hh:hh + 1, :]).astype(BF16))
            wms.append((b_gt * w_t[hh:hh + 1, :]).astype(BF16))
            colvs.append(colv)
        yd = jnp.dot(jnp.concatenate(ms, axis=1), x_cat, preferred_element_type=F32)
        sn = jnp.dot(jnp.concatenate(wms, axis=1), x_cat, preferred_element_type=F32)
        e_pair = jnp.exp(jnp.where(low_half, colvs[0], colvs[1]))
        issued[p] = (yd, sn, e_pair, off, skip)

    def pair_finish(p):
        yd, sn, e_pair, off, skip = issued.pop(p)
        h0 = 2 * p
        ps = slice(h0 * SSD_HEAD_DIM, (h0 + 2) * SSD_HEAD_DIM)
        ybuf[:, ps] = yd + e_pair * off + skip
        cd = jnp.where(low_half[0:1, :], chunk_decay[:, h0:h0 + 1], chunk_decay[:, h0 + 1:h0 + 2])
        state[:, ps] = state[:, ps] * cd + sn

    def gate_group(g):
        gwid = SSD_WIDTH // SSD_GROUPS
        rows = 32
        gs = slice(g * gwid, (g + 1) * gwid)
        for r in range(t // rows):
            rs = slice(r * rows, (r + 1) * rows)
            yg = ybuf[rs, gs] * _silu(p_ref[rs, gs].astype(F32))
            ms = jnp.mean(yg * yg, axis=-1, keepdims=True)
            o_ref[rs, gs] = (yg * lax.rsqrt(ms + EPS) * nw_ref[:, gs]).astype(o_ref.dtype)

    for p in range(n_pairs):
        pair_prepare(p)
    chain_ticks = {SSD_CHAIN_TICKS[0]: lambda: chain_project(hn_ref)}
    chain_ticks.update(zip(SSD_CHAIN_TICKS[1:], chain_stages))
    for tick in range(n_pairs + SSD_PAIR_SKEW):
        if tick in chain_ticks:
            chain_ticks[tick]()
        if tick < n_pairs:
            pair_issue(tick)
        done = tick - SSD_PAIR_SKEW
        if done >= 0:
            pair_finish(done)
            if done % pairs_per_group == pairs_per_group - 1:
                gate_group(done // pairs_per_group)


def ssd_group(p_ssd, h, layer, w_t, conv_w, conv_b, dtb, alog, dsk, norm_w):
    seq = p_ssd.shape[0]
    t = SSD_T
    assert SSD_COLS % SSD_HEADS == 0
    per_layer = lambda *shape: pl.BlockSpec((None,) + shape, lambda i: (layer,) + (0,) * len(shape))
    return pl.pallas_call(
        _ssd_kernel,
        grid=(seq // t,),
        in_specs=[pl.BlockSpec((t, SSD_COLS), lambda i: (i, 0)),
                  pl.BlockSpec((t, D_MODEL), lambda i: (0, 0)),
                  pl.BlockSpec((t, D_MODEL), lambda i: (jnp.minimum(i + 1, seq // t - 1), 0)),
                  pl.BlockSpec((None, SSD_HEADS, D_MODEL), lambda i: (layer, SSD_COLS // SSD_HEADS, 0)),
                  per_layer(SSD_CONV, CONV_DIM),
                  per_layer(1, CONV_DIM),
                  per_layer(1, LANES),
                  per_layer(1, LANES),
                  per_layer(1, SSD_WIDTH),
                  per_layer(1, SSD_WIDTH)],
        out_specs=pl.BlockSpec((t, SSD_WIDTH), lambda i: (i, 0)),
        out_shape=jax.ShapeDtypeStruct((seq, SSD_WIDTH), BF16),
        scratch_shapes=[pltpu.VMEM((SUBLANES, CONV_DIM), F32),
                        pltpu.VMEM((t, CONV_DIM), F32),
                        pltpu.VMEM((SSD_STATE, SSD_WIDTH), F32),
                        pltpu.VMEM((t, SSD_WIDTH), F32),
                        pltpu.VMEM((LANES, D_MODEL), BF16),
                        pltpu.VMEM((4, t, LANES), F32)],
        compiler_params=_cparams(("arbitrary",)),
        name="ssd_group",
    )(p_ssd, h, h, w_t, conv_w, conv_b, dtb, alog, dsk, norm_w)


def _sba_kernel(q_ref, k_ref, v_ref, g_ref, o_ref, c_ref, acc_ref):
    rows = SBA_ROWS
    kw = SBA_KEYS
    qt = q_ref.shape[1]
    n_sub = qt // rows
    subs = range(n_sub)
    qi = pl.program_id(1)
    row = lax.broadcasted_iota(jnp.int32, (rows, kw), 0)
    col = lax.broadcasted_iota(jnp.int32, (rows, kw), 1)
    col_minus_row = col - row
    r2 = lax.broadcasted_iota(jnp.int32, (SBA_SUM_PARTS * kw, 2 * kw), 0) % kw
    c2 = lax.broadcasted_iota(jnp.int32, (SBA_SUM_PARTS * kw, 2 * kw), 1)
    neg_cum = jnp.where((c2 >= kw) | (r2 > c2), -1.0, 0.0).astype(BF16)

    def sweep(ds, opening):
        grp = SBA_CUM_GROUP
        units = [(i, g0) for g0 in range(0, n_sub, grp) for i in range(len(ds))]
        last_i = len(ds) - 1
        zs, masks, sps, packed, sums, ws_b, kss, starts = {}, {}, {}, {}, {}, {}, {}, {}
        carries = {}
        wants = []

        def stage_scores(i, g0):
            for s in range(g0, g0 + grp):
                q0 = qi * qt + s * rows
                start = q0 + rows - (ds[i] + 1) * kw
                load = jnp.maximum(start, 0)
                starts[i, s] = start
                kss[i, s] = pl.ds(pl.multiple_of(load, rows), kw)
                zs[i, s] = lax.dot_general(q_ref[0, s * rows:(s + 1) * rows, :], k_ref[0, kss[i, s], :], NT_DIMS,
                                           preferred_element_type=F32)
                if opening and i == 0:
                    masks[i, s] = col_minus_row < (q0 - load)
                else:
                    masks[i, s] = col < (start + kw - load)

        def stage_softplus(i, g0):
            parts = []
            for s in range(g0, g0 + grp):
                sps[i, s] = jnp.where(masks[i, s], _softplus(zs[i, s]), 0.0)
                parts.append(jnp.concatenate(_split_bf16(sps[i, s], SBA_SUM_PARTS), axis=1))
            packed[i, g0] = jnp.concatenate(parts, axis=0)

        def stage_sums(i, g0):
            both = jnp.dot(packed.pop((i, g0)), neg_cum, preferred_element_type=F32)
            for j, s in enumerate(range(g0, g0 + grp)):
                sums[i, s] = both[j * rows:(j + 1) * rows]

        def stage_weights(i, g0):
            for s in range(g0, g0 + grp):
                if i == 0:
                    carries[s] = None if opening else c_ref[s]
                logw = zs[i, s] - sps[i, s] + sums[i, s][:, :kw]
                if carries[s] is not None:
                    logw = logw + carries[s]
                ws_b[i, s] = jnp.where(masks[i, s], jnp.exp(logw), 0.0).astype(BF16)
                tot = sums[i, s][:, kw:]
                carries[s] = tot if carries[s] is None else carries[s] + tot
                if i == last_i:
                    c_ref[s] = carries[s]
                    wants.append(jnp.where(starts[i, s] > 0, carries[s], -jnp.inf))

        def stage_values(i, g0):
            for s in range(g0, g0 + grp):
                pv = jnp.dot(ws_b[i, s], v_ref[0, kss[i, s], :], preferred_element_type=F32)
                if opening and i == 0:
                    acc_ref[s] = pv
                else:
                    acc_ref[s] += pv

        stages = (stage_scores, stage_softplus, stage_sums, stage_weights, stage_values)
        skew = SBA_STAGE_SKEW
        for tick in range(len(units) + (len(stages) - 1) * skew):
            for j in reversed(range(len(stages))):
                u = tick - j * skew
                if 0 <= u < len(units):
                    stages[j](*units[u])
        pending = wants[0]
        for want in wants[1:]:
            pending = jnp.maximum(pending, want)
        return jnp.max(pending)

    n_open = SBA_OPENING_STEPS
    first = sweep(list(range(n_open)), True)
    lax.while_loop(lambda st: st[1] > SBA_EXIT,
                   lambda st: (st[0] + 1, sweep([st[0]], False)),
                   (jnp.int32(n_open), first))
    for s in subs:
        qs = slice(s * rows, (s + 1) * rows)
        o_ref[qs, :] = (acc_ref[s] * _silu(g_ref[0, qs, :].astype(F32))).astype(o_ref.dtype)


def sba_group(p_sba):
    seq = p_sba.shape[1]
    qt = min(SBA_QT, seq)
    nh = SBA_HEADS
    return pl.pallas_call(
        _sba_kernel,
        grid=(nh, seq // qt),
        in_specs=[pl.BlockSpec((1, qt, SBA_DIM), lambda h, i: (h, i, 0)),
                  pl.BlockSpec((1, seq, SBA_DIM), lambda h, i: (nh + h, 0, 0)),
                  pl.BlockSpec((1, seq, SBA_DIM), lambda h, i: (2 * nh + h, 0, 0)),
                  pl.BlockSpec((1, qt, SBA_DIM), lambda h, i: (3 * nh + h, i, 0))],
        out_specs=pl.BlockSpec((qt, SBA_DIM), lambda h, i: (i, h)),
        out_shape=jax.ShapeDtypeStruct((seq, SBA_WIDTH), BF16),
        scratch_shapes=[pltpu.VMEM((qt // SBA_ROWS, SBA_ROWS, SBA_KEYS), F32),
                        pltpu.VMEM((qt // SBA_ROWS, SBA_ROWS, SBA_DIM), F32)],
        compiler_params=_cparams(("arbitrary", "arbitrary")),
        name="sba_group",
    )(p_sba, p_sba, p_sba, p_sba)


def _out_proj_kernel(ya_ref, yb_ref, wa_ref, wb_ref, x_ref, nw_ref, *out_refs, emit_x):
    acc = jnp.dot(ya_ref[...], wa_ref[...], preferred_element_type=F32)
    acc = acc + jnp.dot(yb_ref[...], wb_ref[...], preferred_element_type=F32)
    x_new = x_ref[...] + acc
    if emit_x:
        xo_ref, h_ref = out_refs
        xo_ref[...] = x_new
    else:
        (h_ref,) = out_refs
    h_ref[...] = _rmsnorm_rows(x_new, nw_ref[...]).astype(h_ref.dtype)


def out_proj(y_ssd, y_sba, w_all, layer, x, norm_all, norm_idx, last, tm=256):
    seq, d = x.shape
    ka, kb = y_ssd.shape[1], y_sba.shape[1]
    assert ka == kb and w_all.shape[1] == ka + kb
    row_spec = lambda width: pl.BlockSpec((tm, width), lambda i: (i, 0))
    if last:
        out_specs = row_spec(d)
        out_shape = jax.ShapeDtypeStruct((seq, d), F32)
    else:
        out_specs = [row_spec(d), row_spec(d)]
        out_shape = [jax.ShapeDtypeStruct((seq, d), F32), jax.ShapeDtypeStruct((seq, d), BF16)]
    return pl.pallas_call(
        functools.partial(_out_proj_kernel, emit_x=not last),
        grid=(seq // tm,),
        in_specs=[row_spec(ka), row_spec(kb),
                  pl.BlockSpec((None, ka, d), lambda i: (layer, 0, 0)),
                  pl.BlockSpec((None, kb, d), lambda i: (layer, 1, 0)),
                  row_spec(d),
                  pl.BlockSpec((None, 1, d), lambda i: (norm_idx, 0, 0))],
        out_specs=out_specs,
        out_shape=out_shape,
        compiler_params=_cparams(("arbitrary",)),
        name="out_proj",
    )(y_ssd, y_sba, w_all, w_all, x, norm_all)


def kernel(x, norm_w, w_in, conv_w, conv_b, dt_bias, a_log, d_skip, ssd_norm_w, w_out, final_norm_w):
    bsz, seq, d = x.shape
    assert bsz == 1 and d == D_MODEL
    depth = w_in.shape[0]
    sba_row0 = SSD_COLS + SSD_HEADS
    pad = LANES - SSD_HEADS
    w_in_t = jnp.swapaxes(w_in, 1, 2)
    w_out_b = w_out.astype(BF16)
    dtb = jnp.pad(dt_bias, ((0, 0), (0, pad))).reshape(depth, 1, LANES)
    alog = jnp.pad(a_log, ((0, 0), (0, pad))).reshape(depth, 1, LANES)
    dsk = jnp.repeat(d_skip, SSD_HEAD_DIM, axis=1).reshape(depth, 1, SSD_WIDTH)
    conv_b3 = conv_b.reshape(depth, 1, CONV_DIM)
    ssd_nw = ssd_norm_w.reshape(depth, 1, SSD_WIDTH)
    norms = jnp.concatenate([norm_w, final_norm_w[None]], axis=0).reshape(depth + 1, 1, d)

    xs = x.reshape(seq, d)
    h = rmsnorm_bf16(xs, norm_w[0])
    for i in range(depth):
        p_ssd = in_proj(h, w_in_t, i, 0, SSD_COLS, False, "in_proj_ssd")
        p_sba = in_proj(h, w_in_t, i, sba_row0, 4 * SBA_WIDTH, True, "in_proj_sba",
                        scaled_cols=SBA_WIDTH, scale=1.0 / math.sqrt(SBA_DIM))
        y_ssd = ssd_group(p_ssd, h, i, w_in_t, conv_w, conv_b3, dtb, alog, dsk, ssd_nw)
        y_sba = sba_group(p_sba)
        last = i == depth - 1
        res = out_proj(y_ssd, y_sba, w_out_b, i, xs, norms, i + 1, last)
        if last:
            out = res
        else:
            xs, h = res
    return out.reshape(bsz, seq, d)
```

```python
import functools
import math

import jax
import jax.numpy as jnp
from jax import lax
from jax.experimental import pallas as pl
from jax.experimental.pallas import tpu as pltpu

F32 = jnp.float32
BF16 = jnp.bfloat16

D_MODEL = 2048
SSD_WIDTH = 2048
SSD_HEAD_DIM = 64
SSD_HEADS = SSD_WIDTH // SSD_HEAD_DIM
SSD_GROUPS = 4
SSD_STATE = 128
SSD_CONV = 4
SSD_BC = SSD_GROUPS * SSD_STATE
CONV_DIM = SSD_WIDTH + 2 * SSD_BC
SSD_COLS = SSD_WIDTH + CONV_DIM
SBA_HEADS = 16
SBA_DIM = 128
SBA_WIDTH = SBA_HEADS * SBA_DIM
EPS = 1e-6

LANES = 128
SUBLANES = 8
VMEM_LIMIT = 56 * 1024 * 1024

SSD_T = 128
CONV_BLK = 256
SSD_PAIR_SKEW = 4
SSD_CHAIN_TICKS = (1, 5, 10, 16)
SBA_ROWS = 64
SBA_KEYS = 128
SBA_CUM_GROUP = 2
SBA_SUM_PARTS = 1
SBA_OPENING_STEPS = 2
SBA_STAGE_SKEW = 4
SBA_QT = 8192
SBA_EXIT = -104.0


def _cparams(sem):
    return pltpu.CompilerParams(dimension_semantics=sem, vmem_limit_bytes=VMEM_LIMIT)


def _silu(a):
    half = 0.5 * a
    return half * jnp.tanh(half) + half


def _softplus(a):
    return jnp.maximum(a, 0.0) + jnp.log(1.0 + jnp.exp(-jnp.abs(a)))


def _split_bf16(a, parts):
    out = []
    r = a
    for i in range(parts):
        p = r.astype(BF16)
        out.append(p)
        if i + 1 < parts:
            r = r - p.astype(F32)
    return out


def _rmsnorm_rows(x, w):
    ms = jnp.mean(x * x, axis=-1, keepdims=True)
    return x * lax.rsqrt(ms + EPS) * w


def _rmsnorm_kernel(x_ref, w_ref, o_ref, *, rows):
    w = w_ref[...]

    def body(r, carry):
        sl = pl.ds(pl.multiple_of(r * rows, rows), rows)
        o_ref[sl, :] = _rmsnorm_rows(x_ref[sl, :], w).astype(o_ref.dtype)
        return carry

    lax.fori_loop(0, x_ref.shape[0] // rows, body, 0)


def rmsnorm_bf16(x, w, tm=1024):
    seq, d = x.shape
    return pl.pallas_call(
        functools.partial(_rmsnorm_kernel, rows=32),
        grid=(seq // tm,),
        in_specs=[pl.BlockSpec((tm, d), lambda i: (i, 0)),
                  pl.BlockSpec((1, d), lambda i: (0, 0))],
        out_specs=pl.BlockSpec((tm, d), lambda i: (i, 0)),
        out_shape=jax.ShapeDtypeStruct((seq, d), BF16),
        compiler_params=_cparams(("arbitrary",)),
        name="rmsnorm",
    )(x, w.reshape(1, d))


NT_DIMS = (((1,), (1,)), ((), ()))
W_ROW_ALIGN = 32


def _proj_kernel(h_ref, *refs, shift, heads, scaled_tiles, scale, side_cast):
    refs = list(refs)
    w_lo_ref = refs.pop(0)
    w_hi_ref = refs.pop(0) if shift else None
    side_in_ref = refs.pop(0) if side_cast else None
    o_ref = refs.pop(0)
    side_out_ref = refs.pop(0) if side_cast else None
    wb_ref = refs.pop(0)
    tn = wb_ref.shape[0]
    if side_cast:
        side_out_ref[...] = side_in_ref[...].astype(side_out_ref.dtype)

    @pl.when(pl.program_id(1) == 0)
    def _():
        rows = W_ROW_ALIGN
        factor = jnp.where(pl.program_id(0) < scaled_tiles, scale, 1.0).astype(F32) if scaled_tiles else None
        stage = (lambda w: (w * factor).astype(BF16)) if scaled_tiles else (lambda w: w.astype(BF16))

        def body(r, carry):
            dst = pl.ds(pl.multiple_of(r * rows, rows), rows)
            src = pl.ds(pl.multiple_of(r * rows + shift, rows), rows)
            wb_ref[dst, :] = stage(w_lo_ref[src, :])
            return carry

        lax.fori_loop(0, (tn - shift) // rows, body, 0)
        if shift:
            wb_ref[tn - shift:tn, :] = stage(w_hi_ref[...])

    acc = lax.dot_general(h_ref[...], wb_ref[...], NT_DIMS, preferred_element_type=F32)
    if heads:
        for hh in range(o_ref.shape[0]):
            o_ref[hh] = acc[:, hh * LANES:(hh + 1) * LANES].astype(o_ref.dtype)
    else:
        o_ref[...] = acc.astype(o_ref.dtype)


def in_proj(h, w_t, layer, row0, n, heads, name, scaled_cols=0, scale=1.0, cast_along=None, tm=1024, tn=1024):
    seq, k = h.shape
    shift = row0 % tn
    assert shift % W_ROW_ALIGN == 0 and n % tn == 0 and w_t.shape[1] >= row0 + n and scaled_cols % tn == 0
    lo0 = row0 // tn
    in_specs = [pl.BlockSpec((tm, k), lambda j, i: (i, 0)),
                pl.BlockSpec((None, tn, k), lambda j, i: (layer, lo0 + j, 0))]
    operands = [h, w_t]
    if shift:
        assert tn % shift == 0
        per = tn // shift
        in_specs.append(pl.BlockSpec((None, shift, k), lambda j, i: (layer, (lo0 + j + 1) * per, 0)))
        operands.append(w_t)
    m_tiles = seq // tm
    if cast_along is not None:
        steps = (n // tn) * m_tiles
        side_rows, side_cols = cast_along.shape[1:]
        assert side_rows % steps == 0 and (side_rows // steps) % W_ROW_ALIGN == 0
        in_specs.append(pl.BlockSpec((None, side_rows // steps, side_cols), lambda j, i: (layer, j * m_tiles + i, 0)))
        operands.append(cast_along)
    if heads:
        hp = tn // LANES
        out_spec = pl.BlockSpec((hp, tm, LANES), lambda j, i: (j, i, 0))
        out_shape = jax.ShapeDtypeStruct((n // LANES, seq, LANES), BF16)
    else:
        out_spec = pl.BlockSpec((tm, tn), lambda j, i: (i, j))
        out_shape = jax.ShapeDtypeStruct((seq, n), BF16)
    if cast_along is not None:
        out_spec = [out_spec, pl.BlockSpec((side_rows // steps, side_cols), lambda j, i: (j * m_tiles + i, 0))]
        out_shape = [out_shape, jax.ShapeDtypeStruct((side_rows, side_cols), BF16)]
    return pl.pallas_call(
        functools.partial(_proj_kernel, shift=shift, heads=heads, scaled_tiles=scaled_cols // tn, scale=scale,
                          side_cast=cast_along is not None),
        grid=(n // tn, m_tiles),
        in_specs=in_specs,
        out_specs=out_spec,
        out_shape=out_shape,
        scratch_shapes=[pltpu.VMEM((tn, k), BF16)],
        compiler_params=_cparams(("arbitrary", "arbitrary")),
        name=name,
    )(*operands)


def _ssd_kernel(p_ref, h0_ref, hn_ref, wdt_ref, cw_ref, cb_ref, dtb_ref, alog_ref, dsk_ref, nw_ref, o_ref,
                hbuf, xbc, state, ybuf, wdt_b, chain):
    t = SSD_T
    hist = SUBLANES
    c = pl.program_id(0)
    row = lax.broadcasted_iota(jnp.int32, (t, t), 0)
    col = lax.broadcasted_iota(jnp.int32, (t, t), 1)
    tril = row >= col
    tril_b = jnp.where(tril, 1.0, 0.0).astype(BF16)
    nxt = {}

    def chain_project(h_in):
        nxt["dt_raw"] = lax.dot_general(h_in[...], wdt_b[...], NT_DIMS, preferred_element_type=F32)

    def chain_cumsum():
        dt = _softplus(nxt["dt_raw"] + dtb_ref[...])
        da = dt * -jnp.exp(alog_ref[...])
        a_cum = None
        for part in _split_bf16(da, 3):
            term = jnp.dot(tril_b, part, preferred_element_type=F32)
            a_cum = term if a_cum is None else a_cum + term
        nxt["dt"], nxt["a_cum"] = dt, a_cum

    def chain_transpose():
        nxt["a_cum_t"] = nxt["a_cum"].T
        nxt["dt_t"] = nxt["dt"].T

    def chain_store():
        a_cum_t, dt_t = nxt["a_cum_t"], nxt["dt_t"]
        a_last_t = jnp.broadcast_to(a_cum_t[:, t - 1:t], a_cum_t.shape)
        chain[0] = nxt["a_cum"]
        chain[1] = a_cum_t
        chain[2] = dt_t
        chain[3] = jnp.exp(a_last_t - a_cum_t) * dt_t

    chain_stages = (chain_cumsum, chain_transpose, chain_store)

    @pl.when(c == 0)
    def _():
        hbuf[...] = jnp.zeros_like(hbuf)
        state[...] = jnp.zeros_like(state)
        wdt_b[0:SSD_HEADS, :] = wdt_ref[...].astype(BF16)
        wdt_b[SSD_HEADS:LANES, :] = jnp.zeros((LANES - SSD_HEADS, D_MODEL), BF16)
        chain_project(h0_ref)
        for stage in chain_stages:
            stage()

    a_cum = chain[0]
    a_cum_t = chain[1]
    dt_t = chain[2]
    w_t = chain[3]
    chunk_decay = jnp.exp(a_cum[t - 1:t, :])

    shift_by = {back: jnp.where(row - col == back, 1.0, 0.0).astype(BF16) for back in range(1, SSD_CONV)}
    top = lax.broadcasted_iota(jnp.int32, (hist, CONV_BLK), 0)

    def conv_blocks(j0, j1):
        for j in range(j0, j1):
            cs = slice(j * CONV_BLK, (j + 1) * CONV_BLK)
            ub = p_ref[:, SSD_WIDTH + j * CONV_BLK:SSD_WIDTH + (j + 1) * CONV_BLK]
            u = ub.astype(F32)
            prev = hbuf[:, cs]
            acc = cb_ref[:, cs] + cw_ref[SSD_CONV - 1:SSD_CONV, cs] * u
            for back in range(1, SSD_CONV):
                sh = jnp.dot(shift_by[back], ub, preferred_element_type=F32)
                head = jnp.where(top < back, pltpu.roll(prev, back, axis=0), sh[0:hist])
                sh = jnp.concatenate([head, sh[hist:]], axis=0)
                acc = acc + cw_ref[SSD_CONV - 1 - back:SSD_CONV - back, cs] * sh
            xbc[:, cs] = _silu(acc)
            hbuf[:, cs] = u[t - hist:t]

    conv_blocks(0, CONV_DIM // CONV_BLK)
    lane = lax.broadcasted_iota(jnp.int32, (t, LANES), 1)
    low_half = lane < SSD_HEAD_DIM

    heads_per_group = SSD_HEADS // SSD_GROUPS
    pairs_per_group = heads_per_group // 2
    n_pairs = SSD_HEADS // 2
    group_vals = {}
    prepared = {}
    issued = {}

    def pair_prepare(p):
        g = p // pairs_per_group
        if p % pairs_per_group == 0:
            b_g = xbc[:, SSD_WIDTH + g * SSD_STATE:SSD_WIDTH + (g + 1) * SSD_STATE]
            c_g = xbc[:, SSD_WIDTH + SSD_BC + g * SSD_STATE:SSD_WIDTH + SSD_BC + (g + 1) * SSD_STATE]
            c_gb = c_g.astype(BF16)
            cb = lax.dot_general(c_gb, b_g.astype(BF16), NT_DIMS, preferred_element_type=F32)
            group_vals[g] = (c_gb, cb, b_g.T)
        c_gb = group_vals[g][0]
        ps = slice(2 * p * SSD_HEAD_DIM, (2 * p + 2) * SSD_HEAD_DIM)
        x_pair = xbc[:, ps]
        x_lo = jnp.where(low_half, x_pair, 0.0).astype(BF16)
        x_hi = jnp.where(low_half, 0.0, x_pair).astype(BF16)
        off = jnp.dot(c_gb, state[:, ps].astype(BF16), preferred_element_type=F32)
        prepared[p] = (jnp.concatenate([x_lo, x_hi], axis=0), off, dsk_ref[:, ps] * x_pair)

    def pair_issue(p):
        _, cb, b_gt = group_vals[p // pairs_per_group]
        h0 = 2 * p
        x_cat, off, skip = prepared.pop(p)
        ms, wms, colvs = [], [], []
        for hh in (h0, h0 + 1):
            colv = jnp.broadcast_to(a_cum[:, hh:hh + 1], (t, t))
            rowv = a_cum_t[hh:hh + 1, :]
            decay = jnp.where(tril, jnp.exp(colv - rowv), 0.0)
            ms.append((cb * decay * dt_t[hh:hh + 1, :]).astype(BF16))
            wms.append((b_gt * w_t[hh:hh + 1, :]).astype(BF16))
            colvs.append(colv)
        yd = jnp.dot(jnp.concatenate(ms, axis=1), x_cat, preferred_element_type=F32)
        sn = jnp.dot(jnp.concatenate(wms, axis=1), x_cat, preferred_element_type=F32)
        e_pair = jnp.exp(jnp.where(low_half, colvs[0], colvs[1]))
        issued[p] = (yd, sn, e_pair, off, skip)

    def pair_finish(p):
        yd, sn, e_pair, off, skip = issued.pop(p)
        h0 = 2 * p
        ps = slice(h0 * SSD_HEAD_DIM, (h0 + 2) * SSD_HEAD_DIM)
        ybuf[:, ps] = yd + e_pair * off + skip
        cd = jnp.where(low_half[0:1, :], chunk_decay[:, h0:h0 + 1], chunk_decay[:, h0 + 1:h0 + 2])
        state[:, ps] = state[:, ps] * cd + sn

    def gate_group(g):
        gwid = SSD_WIDTH // SSD_GROUPS
        rows = 32
        gs = slice(g * gwid, (g + 1) * gwid)
        for r in range(t // rows):
            rs = slice(r * rows, (r + 1) * rows)
            yg = ybuf[rs, gs] * _silu(p_ref[rs, gs].astype(F32))
            ms = jnp.mean(yg * yg, axis=-1, keepdims=True)
            o_ref[rs, gs] = (yg * lax.rsqrt(ms + EPS) * nw_ref[:, gs]).astype(o_ref.dtype)

    for p in range(n_pairs):
        pair_prepare(p)
    chain_ticks = {SSD_CHAIN_TICKS[0]: lambda: chain_project(hn_ref)}
    chain_ticks.update(zip(SSD_CHAIN_TICKS[1:], chain_stages))
    for tick in range(n_pairs + SSD_PAIR_SKEW):
        if tick in chain_ticks:
            chain_ticks[tick]()
        if tick < n_pairs:
            pair_issue(tick)
        done = tick - SSD_PAIR_SKEW
        if done >= 0:
            pair_finish(done)
            if done % pairs_per_group == pairs_per_group - 1:
                gate_group(done // pairs_per_group)


def ssd_group(p_ssd, h, layer, w_t, conv_w, conv_b, dtb, alog, dsk, norm_w):
    seq = p_ssd.shape[0]
    t = SSD_T
    assert SSD_COLS % SSD_HEADS == 0
    per_layer = lambda *shape: pl.BlockSpec((None,) + shape, lambda i: (layer,) + (0,) * len(shape))
    return pl.pallas_call(
        _ssd_kernel,
        grid=(seq // t,),
        in_specs=[pl.BlockSpec((t, SSD_COLS), lambda i: (i, 0)),
                  pl.BlockSpec((t, D_MODEL), lambda i: (0, 0)),
                  pl.BlockSpec((t, D_MODEL), lambda i: (jnp.minimum(i + 1, seq // t - 1), 0)),
                  pl.BlockSpec((None, SSD_HEADS, D_MODEL), lambda i: (layer, SSD_COLS // SSD_HEADS, 0)),
                  per_layer(SSD_CONV, CONV_DIM),
                  per_layer(1, CONV_DIM),
                  per_layer(1, LANES),
                  per_layer(1, LANES),
                  per_layer(1, SSD_WIDTH),
                  per_layer(1, SSD_WIDTH)],
        out_specs=pl.BlockSpec((t, SSD_WIDTH), lambda i: (i, 0)),
        out_shape=jax.ShapeDtypeStruct((seq, SSD_WIDTH), BF16),
        scratch_shapes=[pltpu.VMEM((SUBLANES, CONV_DIM), F32),
                        pltpu.VMEM((t, CONV_DIM), F32),
                        pltpu.VMEM((SSD_STATE, SSD_WIDTH), F32),
                        pltpu.VMEM((t, SSD_WIDTH), F32),
                        pltpu.VMEM((LANES, D_MODEL), BF16),
                        pltpu.VMEM((4, t, LANES), F32)],
        compiler_params=_cparams(("arbitrary",)),
        name="ssd_group",
    )(p_ssd, h, h, w_t, conv_w, conv_b, dtb, alog, dsk, norm_w)


def _sba_kernel(q_ref, k_ref, v_ref, g_ref, o_ref, c_ref, acc_ref):
    rows = SBA_ROWS
    kw = SBA_KEYS
    qt = q_ref.shape[1]
    n_sub = qt // rows
    subs = range(n_sub)
    qi = pl.program_id(1)
    row = lax.broadcasted_iota(jnp.int32, (rows, kw), 0)
    col = lax.broadcasted_iota(jnp.int32, (rows, kw), 1)
    col_minus_row = col - row
    r2 = lax.broadcasted_iota(jnp.int32, (SBA_SUM_PARTS * kw, 2 * kw), 0) % kw
    c2 = lax.broadcasted_iota(jnp.int32, (SBA_SUM_PARTS * kw, 2 * kw), 1)
    neg_cum = jnp.where((c2 >= kw) | (r2 > c2), -1.0, 0.0).astype(BF16)

    def sweep(ds, opening):
        grp = SBA_CUM_GROUP
        units = [(i, g0) for g0 in range(0, n_sub, grp) for i in range(len(ds))]
        last_i = len(ds) - 1
        zs, masks, sps, packed, sums, ws_b, kss, starts = {}, {}, {}, {}, {}, {}, {}, {}
        carries = {}
        wants = []

        def stage_scores(i, g0):
            for s in range(g0, g0 + grp):
                q0 = qi * qt + s * rows
                start = q0 + rows - (ds[i] + 1) * kw
                load = jnp.maximum(start, 0)
                starts[i, s] = start
                kss[i, s] = pl.ds(pl.multiple_of(load, rows), kw)
                zs[i, s] = lax.dot_general(q_ref[0, s * rows:(s + 1) * rows, :], k_ref[0, kss[i, s], :], NT_DIMS,
                                           preferred_element_type=F32)
                if opening and i == 0:
                    masks[i, s] = col_minus_row < (q0 - load)
                else:
                    masks[i, s] = col < (start + kw - load)

        def stage_softplus(i, g0):
            parts = []
            for s in range(g0, g0 + grp):
                sps[i, s] = jnp.where(masks[i, s], _softplus(zs[i, s]), 0.0)
                parts.append(jnp.concatenate(_split_bf16(sps[i, s], SBA_SUM_PARTS), axis=1))
            packed[i, g0] = jnp.concatenate(parts, axis=0)

        def stage_sums(i, g0):
            both = jnp.dot(packed.pop((i, g0)), neg_cum, preferred_element_type=F32)
            for j, s in enumerate(range(g0, g0 + grp)):
                sums[i, s] = both[j * rows:(j + 1) * rows]

        def stage_weights(i, g0):
            for s in range(g0, g0 + grp):
                if i == 0:
                    carries[s] = None if opening else c_ref[s]
                logw = zs[i, s] - sps[i, s] + sums[i, s][:, :kw]
                if carries[s] is not None:
                    logw = logw + carries[s]
                ws_b[i, s] = jnp.where(masks[i, s], jnp.exp(logw), 0.0).astype(BF16)
                tot = sums[i, s][:, kw:]
                carries[s] = tot if carries[s] is None else carries[s] + tot
                if i == last_i:
                    c_ref[s] = carries[s]
                    wants.append(jnp.where(starts[i, s] > 0, carries[s], -jnp.inf))

        def stage_values(i, g0):
            for s in range(g0, g0 + grp):
                pv = jnp.dot(ws_b[i, s], v_ref[0, kss[i, s], :], preferred_element_type=F32)
                if opening and i == 0:
                    acc_ref[s] = pv
                else:
                    acc_ref[s] += pv

        stages = (stage_scores, stage_softplus, stage_sums, stage_weights, stage_values)
        skew = SBA_STAGE_SKEW
        for tick in range(len(units) + (len(stages) - 1) * skew):
            for j in reversed(range(len(stages))):
                u = tick - j * skew
                if 0 <= u < len(units):
                    stages[j](*units[u])
        pending = wants[0]
        for want in wants[1:]:
            pending = jnp.maximum(pending, want)
        return jnp.max(pending)

    n_open = SBA_OPENING_STEPS
    first = sweep(list(range(n_open)), True)
    lax.while_loop(lambda st: st[1] > SBA_EXIT,
                   lambda st: (st[0] + 1, sweep([st[0]], False)),
                   (jnp.int32(n_open), first))
    for s in subs:
        qs = slice(s * rows, (s + 1) * rows)
        o_ref[qs, :] = (acc_ref[s] * _silu(g_ref[0, qs, :].astype(F32))).astype(o_ref.dtype)


def sba_group(p_sba):
    seq = p_sba.shape[1]
    qt = min(SBA_QT, seq)
    nh = SBA_HEADS
    return pl.pallas_call(
        _sba_kernel,
        grid=(nh, seq // qt),
        in_specs=[pl.BlockSpec((1, qt, SBA_DIM), lambda h, i: (h, i, 0)),
                  pl.BlockSpec((1, seq, SBA_DIM), lambda h, i: (nh + h, 0, 0)),
                  pl.BlockSpec((1, seq, SBA_DIM), lambda h, i: (2 * nh + h, 0, 0)),
                  pl.BlockSpec((1, qt, SBA_DIM), lambda h, i: (3 * nh + h, i, 0))],
        out_specs=pl.BlockSpec((qt, SBA_DIM), lambda h, i: (i, h)),
        out_shape=jax.ShapeDtypeStruct((seq, SBA_WIDTH), BF16),
        scratch_shapes=[pltpu.VMEM((qt // SBA_ROWS, SBA_ROWS, SBA_KEYS), F32),
                        pltpu.VMEM((qt // SBA_ROWS, SBA_ROWS, SBA_DIM), F32)],
        compiler_params=_cparams(("arbitrary", "arbitrary")),
        name="sba_group",
    )(p_sba, p_sba, p_sba, p_sba)


def _out_proj_kernel(ya_ref, yb_ref, wa_ref, wb_ref, x_ref, nw_ref, *out_refs, emit_x):
    acc = jnp.dot(ya_ref[...], wa_ref[...], preferred_element_type=F32)
    acc = acc + jnp.dot(yb_ref[...], wb_ref[...], preferred_element_type=F32)
    x_new = x_ref[...] + acc
    if emit_x:
        xo_ref, h_ref = out_refs
        xo_ref[...] = x_new
    else:
        (h_ref,) = out_refs
    h_ref[...] = _rmsnorm_rows(x_new, nw_ref[...]).astype(h_ref.dtype)


def out_proj(y_ssd, y_sba, w, x, norm_all, norm_idx, last, tm=256):
    seq, d = x.shape
    ka, kb = y_ssd.shape[1], y_sba.shape[1]
    assert ka == kb and w.shape == (ka + kb, d)
    row_spec = lambda width: pl.BlockSpec((tm, width), lambda i: (i, 0))
    if last:
        out_specs = row_spec(d)
        out_shape = jax.ShapeDtypeStruct((seq, d), F32)
    else:
        out_specs = [row_spec(d), row_spec(d)]
        out_shape = [jax.ShapeDtypeStruct((seq, d), F32), jax.ShapeDtypeStruct((seq, d), BF16)]
    return pl.pallas_call(
        functools.partial(_out_proj_kernel, emit_x=not last),
        grid=(seq // tm,),
        in_specs=[row_spec(ka), row_spec(kb),
                  pl.BlockSpec((ka, d), lambda i: (0, 0)),
                  pl.BlockSpec((kb, d), lambda i: (1, 0)),
                  row_spec(d),
                  pl.BlockSpec((None, 1, d), lambda i: (norm_idx, 0, 0))],
        out_specs=out_specs,
        out_shape=out_shape,
        compiler_params=_cparams(("arbitrary",)),
        name="out_proj",
    )(y_ssd, y_sba, w, w, x, norm_all)


def kernel(x, norm_w, w_in, conv_w, conv_b, dt_bias, a_log, d_skip, ssd_norm_w, w_out, final_norm_w):
    bsz, seq, d = x.shape
    assert bsz == 1 and d == D_MODEL
    depth = w_in.shape[0]
    sba_row0 = SSD_COLS + SSD_HEADS
    pad = LANES - SSD_HEADS
    w_in_t = jnp.swapaxes(w_in, 1, 2)
    dtb = jnp.pad(dt_bias, ((0, 0), (0, pad))).reshape(depth, 1, LANES)
    alog = jnp.pad(a_log, ((0, 0), (0, pad))).reshape(depth, 1, LANES)
    dsk = jnp.repeat(d_skip, SSD_HEAD_DIM, axis=1).reshape(depth, 1, SSD_WIDTH)
    conv_b3 = conv_b.reshape(depth, 1, CONV_DIM)
    ssd_nw = ssd_norm_w.reshape(depth, 1, SSD_WIDTH)
    norms = jnp.concatenate([norm_w, final_norm_w[None]], axis=0).reshape(depth + 1, 1, d)

    xs = x.reshape(seq, d)
    h = rmsnorm_bf16(xs, norm_w[0])
    for i in range(depth):
        p_ssd = in_proj(h, w_in_t, i, 0, SSD_COLS, False, "in_proj_ssd")
        p_sba, w_out_b = in_proj(h, w_in_t, i, sba_row0, 4 * SBA_WIDTH, True, "in_proj_sba",
                                 scaled_cols=SBA_WIDTH, scale=1.0 / math.sqrt(SBA_DIM), cast_along=w_out)
        y_ssd = ssd_group(p_ssd, h, i, w_in_t, conv_w, conv_b3, dtb, alog, dsk, ssd_nw)
        y_sba = sba_group(p_sba)
        last = i == depth - 1
        res = out_proj(y_ssd, y_sba, w_out_b, xs, norms, i + 1, last)
        if last:
            out = res
        else:
            xs, h = res
    return out.reshape(bsz, seq, d)
```

```python
import functools
import math

import jax
import jax.numpy as jnp
from jax import lax
from jax.experimental import pallas as pl
from jax.experimental.pallas import tpu as pltpu

F32 = jnp.float32
BF16 = jnp.bfloat16

D_MODEL = 2048
SSD_WIDTH = 2048
SSD_HEAD_DIM = 64
SSD_HEADS = SSD_WIDTH // SSD_HEAD_DIM
SSD_GROUPS = 4
SSD_STATE = 128
SSD_CONV = 4
SSD_BC = SSD_GROUPS * SSD_STATE
CONV_DIM = SSD_WIDTH + 2 * SSD_BC
SSD_COLS = SSD_WIDTH + CONV_DIM
SBA_HEADS = 16
SBA_DIM = 128
SBA_WIDTH = SBA_HEADS * SBA_DIM
EPS = 1e-6

LANES = 128
SUBLANES = 8
VMEM_LIMIT = 56 * 1024 * 1024

SSD_T = 128
CONV_BLK = 256
SSD_PAIR_SKEW = 4
SSD_CHAIN_TICKS = (1, 5, 10, 16)
SBA_ROWS = 64
SBA_KEYS = 128
SBA_CUM_GROUP = 2
SBA_SUM_PARTS = 1
SBA_OPENING_STEPS = 2
SBA_STAGE_SKEW = 4
SBA_QT = 4096
SBA_EXIT = -104.0


def _cparams(sem):
    return pltpu.CompilerParams(dimension_semantics=sem, vmem_limit_bytes=VMEM_LIMIT)


def _silu(a):
    half = 0.5 * a
    return half * jnp.tanh(half) + half


def _softplus(a):
    return jnp.maximum(a, 0.0) + jnp.log(1.0 + jnp.exp(-jnp.abs(a)))


def _split_bf16(a, parts):
    out = []
    r = a
    for i in range(parts):
        p = r.astype(BF16)
        out.append(p)
        if i + 1 < parts:
            r = r - p.astype(F32)
    return out


def _rmsnorm_rows(x, w):
    ms = jnp.mean(x * x, axis=-1, keepdims=True)
    return x * lax.rsqrt(ms + EPS) * w


def _rmsnorm_kernel(x_ref, w_ref, o_ref, *, rows):
    w = w_ref[...]

    def body(r, carry):
        sl = pl.ds(pl.multiple_of(r * rows, rows), rows)
        o_ref[sl, :] = _rmsnorm_rows(x_ref[sl, :], w).astype(o_ref.dtype)
        return carry

    lax.fori_loop(0, x_ref.shape[0] // rows, body, 0)


def rmsnorm_bf16(x, w, tm=1024):
    seq, d = x.shape
    return pl.pallas_call(
        functools.partial(_rmsnorm_kernel, rows=32),
        grid=(seq // tm,),
        in_specs=[pl.BlockSpec((tm, d), lambda i: (i, 0)),
                  pl.BlockSpec((1, d), lambda i: (0, 0))],
        out_specs=pl.BlockSpec((tm, d), lambda i: (i, 0)),
        out_shape=jax.ShapeDtypeStruct((seq, d), BF16),
        compiler_params=_cparams(("arbitrary",)),
        name="rmsnorm",
    )(x, w.reshape(1, d))


NT_DIMS = (((1,), (1,)), ((), ()))
W_ROW_ALIGN = 32


def _proj_kernel(h_ref, *refs, shift, heads, scaled_tiles, scale, side_cast):
    refs = list(refs)
    w_lo_ref = refs.pop(0)
    w_hi_ref = refs.pop(0) if shift else None
    side_in_ref = refs.pop(0) if side_cast else None
    o_ref = refs.pop(0)
    side_out_ref = refs.pop(0) if side_cast else None
    wb_ref = refs.pop(0)
    tn = wb_ref.shape[0]
    if side_cast:
        side_out_ref[...] = side_in_ref[...].astype(side_out_ref.dtype)

    @pl.when(pl.program_id(1) == 0)
    def _():
        rows = W_ROW_ALIGN
        factor = jnp.where(pl.program_id(0) < scaled_tiles, scale, 1.0).astype(F32) if scaled_tiles else None
        stage = (lambda w: (w * factor).astype(BF16)) if scaled_tiles else (lambda w: w.astype(BF16))

        def body(r, carry):
            dst = pl.ds(pl.multiple_of(r * rows, rows), rows)
            src = pl.ds(pl.multiple_of(r * rows + shift, rows), rows)
            wb_ref[dst, :] = stage(w_lo_ref[src, :])
            return carry

        lax.fori_loop(0, (tn - shift) // rows, body, 0)
        if shift:
            wb_ref[tn - shift:tn, :] = stage(w_hi_ref[...])

    acc = lax.dot_general(h_ref[...], wb_ref[...], NT_DIMS, preferred_element_type=F32)
    if heads:
        for hh in range(o_ref.shape[0]):
            o_ref[hh] = acc[:, hh * LANES:(hh + 1) * LANES].astype(o_ref.dtype)
    else:
        o_ref[...] = acc.astype(o_ref.dtype)


def in_proj(h, w_t, layer, row0, n, heads, name, scaled_cols=0, scale=1.0, cast_along=None, tm=1024, tn=1024):
    seq, k = h.shape
    shift = row0 % tn
    assert shift % W_ROW_ALIGN == 0 and n % tn == 0 and w_t.shape[1] >= row0 + n and scaled_cols % tn == 0
    lo0 = row0 // tn
    in_specs = [pl.BlockSpec((tm, k), lambda j, i: (i, 0)),
                pl.BlockSpec((None, tn, k), lambda j, i: (layer, lo0 + j, 0))]
    operands = [h, w_t]
    if shift:
        assert tn % shift == 0
        per = tn // shift
        in_specs.append(pl.BlockSpec((None, shift, k), lambda j, i: (layer, (lo0 + j + 1) * per, 0)))
        operands.append(w_t)
    m_tiles = seq // tm
    if cast_along is not None:
        steps = (n // tn) * m_tiles
        side_rows, side_cols = cast_along.shape[1:]
        assert side_rows % steps == 0 and (side_rows // steps) % W_ROW_ALIGN == 0
        in_specs.append(pl.BlockSpec((None, side_rows // steps, side_cols), lambda j, i: (layer, j * m_tiles + i, 0)))
        operands.append(cast_along)
    if heads:
        hp = tn // LANES
        out_spec = pl.BlockSpec((hp, tm, LANES), lambda j, i: (j, i, 0))
        out_shape = jax.ShapeDtypeStruct((n // LANES, seq, LANES), BF16)
    else:
        out_spec = pl.BlockSpec((tm, tn), lambda j, i: (i, j))
        out_shape = jax.ShapeDtypeStruct((seq, n), BF16)
    if cast_along is not None:
        out_spec = [out_spec, pl.BlockSpec((side_rows // steps, side_cols), lambda j, i: (j * m_tiles + i, 0))]
        out_shape = [out_shape, jax.ShapeDtypeStruct((side_rows, side_cols), BF16)]
    return pl.pallas_call(
        functools.partial(_proj_kernel, shift=shift, heads=heads, scaled_tiles=scaled_cols // tn, scale=scale,
                          side_cast=cast_along is not None),
        grid=(n // tn, m_tiles),
        in_specs=in_specs,
        out_specs=out_spec,
        out_shape=out_shape,
        scratch_shapes=[pltpu.VMEM((tn, k), BF16)],
        compiler_params=_cparams(("arbitrary", "arbitrary")),
        name=name,
    )(*operands)


def _ssd_kernel(p_ref, h0_ref, hn_ref, wdt_ref, cw_ref, cb_ref, dtb_ref, alog_ref, dsk_ref, nw_ref, o_ref,
                hbuf, xbc, state, ybuf, wdt_b, chain):
    t = SSD_T
    hist = SUBLANES
    c = pl.program_id(0)
    row = lax.broadcasted_iota(jnp.int32, (t, t), 0)
    col = lax.broadcasted_iota(jnp.int32, (t, t), 1)
    tril = row >= col
    tril_b = jnp.where(tril, 1.0, 0.0).astype(BF16)
    nxt = {}

    def chain_project(h_in):
        nxt["dt_raw"] = lax.dot_general(h_in[...], wdt_b[...], NT_DIMS, preferred_element_type=F32)

    def chain_cumsum():
        dt = _softplus(nxt["dt_raw"] + dtb_ref[...])
        da = dt * -jnp.exp(alog_ref[...])
        a_cum = None
        for part in _split_bf16(da, 3):
            term = jnp.dot(tril_b, part, preferred_element_type=F32)
            a_cum = term if a_cum is None else a_cum + term
        nxt["dt"], nxt["a_cum"] = dt, a_cum

    def chain_transpose():
        nxt["a_cum_t"] = nxt["a_cum"].T
        nxt["dt_t"] = nxt["dt"].T

    def chain_store():
        a_cum_t, dt_t = nxt["a_cum_t"], nxt["dt_t"]
        a_last_t = jnp.broadcast_to(a_cum_t[:, t - 1:t], a_cum_t.shape)
        chain[0] = nxt["a_cum"]
        chain[1] = a_cum_t
        chain[2] = dt_t
        chain[3] = jnp.exp(a_last_t - a_cum_t) * dt_t

    chain_stages = (chain_cumsum, chain_transpose, chain_store)

    @pl.when(c == 0)
    def _():
        hbuf[...] = jnp.zeros_like(hbuf)
        state[...] = jnp.zeros_like(state)
        wdt_b[0:SSD_HEADS, :] = wdt_ref[...].astype(BF16)
        wdt_b[SSD_HEADS:LANES, :] = jnp.zeros((LANES - SSD_HEADS, D_MODEL), BF16)
        chain_project(h0_ref)
        for stage in chain_stages:
            stage()

    a_cum = chain[0]
    a_cum_t = chain[1]
    dt_t = chain[2]
    w_t = chain[3]
    chunk_decay = jnp.exp(a_cum[t - 1:t, :])

    shift_by = {back: jnp.where(row - col == back, 1.0, 0.0).astype(BF16) for back in range(1, SSD_CONV)}
    top = lax.broadcasted_iota(jnp.int32, (hist, CONV_BLK), 0)

    def conv_blocks(j0, j1):
        for j in range(j0, j1):
            cs = slice(j * CONV_BLK, (j + 1) * CONV_BLK)
            ub = p_ref[:, SSD_WIDTH + j * CONV_BLK:SSD_WIDTH + (j + 1) * CONV_BLK]
            u = ub.astype(F32)
            prev = hbuf[:, cs]
            acc = cb_ref[:, cs] + cw_ref[SSD_CONV - 1:SSD_CONV, cs] * u
            for back in range(1, SSD_CONV):
                sh = jnp.dot(shift_by[back], ub, preferred_element_type=F32)
                head = jnp.where(top < back, pltpu.roll(prev, back, axis=0), sh[0:hist])
                sh = jnp.concatenate([head, sh[hist:]], axis=0)
                acc = acc + cw_ref[SSD_CONV - 1 - back:SSD_CONV - back, cs] * sh
            xbc[:, cs] = _silu(acc)
            hbuf[:, cs] = u[t - hist:t]

    conv_blocks(0, CONV_DIM // CONV_BLK)
    lane = lax.broadcasted_iota(jnp.int32, (t, LANES), 1)
    low_half = lane < SSD_HEAD_DIM

    heads_per_group = SSD_HEADS // SSD_GROUPS
    pairs_per_group = heads_per_group // 2
    n_pairs = SSD_HEADS // 2
    group_vals = {}
    prepared = {}
    issued = {}

    def pair_prepare(p):
        g = p // pairs_per_group
        if p % pairs_per_group == 0:
            b_g = xbc[:, SSD_WIDTH + g * SSD_STATE:SSD_WIDTH + (g + 1) * SSD_STATE]
            c_g = xbc[:, SSD_WIDTH + SSD_BC + g * SSD_STATE:SSD_WIDTH + SSD_BC + (g + 1) * SSD_STATE]
            c_gb = c_g.astype(BF16)
            cb = lax.dot_general(c_gb, b_g.astype(BF16), NT_DIMS, preferred_element_type=F32)
            group_vals[g] = (c_gb, cb, b_g.T)
        c_gb = group_vals[g][0]
        ps = slice(2 * p * SSD_HEAD_DIM, (2 * p + 2) * SSD_HEAD_DIM)
        x_pair = xbc[:, ps]
        x_lo = jnp.where(low_half, x_pair, 0.0).astype(BF16)
        x_hi = jnp.where(low_half, 0.0, x_pair).astype(BF16)
        off = jnp.dot(c_gb, state[:, ps].astype(BF16), preferred_element_type=F32)
        prepared[p] = (jnp.concatenate([x_lo, x_hi], axis=0), off, dsk_ref[:, ps] * x_pair)

    def pair_issue(p):
        _, cb, b_gt = group_vals[p // pairs_per_group]
        h0 = 2 * p
        x_cat, off, skip = prepared.pop(p)
        ms, wms, colvs = [], [], []
        for hh in (h0, h0 + 1):
            colv = jnp.broadcast_to(a_cum[:, hh:hh + 1], (t, t))
            rowv = a_cum_t[hh:hh + 1, :]
            decay = jnp.where(tril, jnp.exp(colv - rowv), 0.0)
            ms.append((cb * decay * dt_t[hh:hh + 1, :]).astype(BF16))
            wms.append((b_gt * w_t[hh:hh + 1, :]).astype(BF16))
            colvs.append(colv)
        yd = jnp.dot(jnp.concatenate(ms, axis=1), x_cat, preferred_element_type=F32)
        sn = jnp.dot(jnp.concatenate(wms, axis=1), x_cat, preferred_element_type=F32)
        e_pair = jnp.exp(jnp.where(low_half, colvs[0], colvs[1]))
        issued[p] = (yd, sn, e_pair, off, skip)

    def pair_finish(p):
        yd, sn, e_pair, off, skip = issued.pop(p)
        h0 = 2 * p
        ps = slice(h0 * SSD_HEAD_DIM, (h0 + 2) * SSD_HEAD_DIM)
        ybuf[:, ps] = yd + e_pair * off + skip
        cd = jnp.where(low_half[0:1, :], chunk_decay[:, h0:h0 + 1], chunk_decay[:, h0 + 1:h0 + 2])
        state[:, ps] = state[:, ps] * cd + sn

    def gate_group(g):
        gwid = SSD_WIDTH // SSD_GROUPS
        rows = 32
        gs = slice(g * gwid, (g + 1) * gwid)
        for r in range(t // rows):
            rs = slice(r * rows, (r + 1) * rows)
            yg = ybuf[rs, gs] * _silu(p_ref[rs, gs].astype(F32))
            ms = jnp.mean(yg * yg, axis=-1, keepdims=True)
            o_ref[rs, gs] = (yg * lax.rsqrt(ms + EPS) * nw_ref[:, gs]).astype(o_ref.dtype)

    for p in range(n_pairs):
        pair_prepare(p)
    chain_ticks = {SSD_CHAIN_TICKS[0]: lambda: chain_project(hn_ref)}
    chain_ticks.update(zip(SSD_CHAIN_TICKS[1:], chain_stages))
    for tick in range(n_pairs + SSD_PAIR_SKEW):
        if tick in chain_ticks:
            chain_ticks[tick]()
        if tick < n_pairs:
            pair_issue(tick)
        done = tick - SSD_PAIR_SKEW
        if done >= 0:
            pair_finish(done)
            if done % pairs_per_group == pairs_per_group - 1:
                gate_group(done // pairs_per_group)


def ssd_group(p_ssd, h, layer, w_t, conv_w, conv_b, dtb, alog, dsk, norm_w):
    seq = p_ssd.shape[0]
    t = SSD_T
    assert SSD_COLS % SSD_HEADS == 0
    per_layer = lambda *shape: pl.BlockSpec((None,) + shape, lambda i: (layer,) + (0,) * len(shape))
    return pl.pallas_call(
        _ssd_kernel,
        grid=(seq // t,),
        in_specs=[pl.BlockSpec((t, SSD_COLS), lambda i: (i, 0)),
                  pl.BlockSpec((t, D_MODEL), lambda i: (0, 0)),
                  pl.BlockSpec((t, D_MODEL), lambda i: (jnp.minimum(i + 1, seq // t - 1), 0)),
                  pl.BlockSpec((None, SSD_HEADS, D_MODEL), lambda i: (layer, SSD_COLS // SSD_HEADS, 0)),
                  per_layer(SSD_CONV, CONV_DIM),
                  per_layer(1, CONV_DIM),
                  per_layer(1, LANES),
                  per_layer(1, LANES),
                  per_layer(1, SSD_WIDTH),
                  per_layer(1, SSD_WIDTH)],
        out_specs=pl.BlockSpec((t, SSD_WIDTH), lambda i: (i, 0)),
        out_shape=jax.ShapeDtypeStruct((seq, SSD_WIDTH), BF16),
        scratch_shapes=[pltpu.VMEM((SUBLANES, CONV_DIM), F32),
                        pltpu.VMEM((t, CONV_DIM), F32),
                        pltpu.VMEM((SSD_STATE, SSD_WIDTH), F32),
                        pltpu.VMEM((t, SSD_WIDTH), F32),
                        pltpu.VMEM((LANES, D_MODEL), BF16),
                        pltpu.VMEM((4, t, LANES), F32)],
        compiler_params=_cparams(("arbitrary",)),
        name="ssd_group",
    )(p_ssd, h, h, w_t, conv_w, conv_b, dtb, alog, dsk, norm_w)


def _sba_kernel(q_ref, k_ref, v_ref, g_ref, o_ref, c_ref, acc_ref):
    rows = SBA_ROWS
    kw = SBA_KEYS
    qt = q_ref.shape[1]
    n_sub = qt // rows
    subs = range(n_sub)
    qi = pl.program_id(1)
    row = lax.broadcasted_iota(jnp.int32, (rows, kw), 0)
    col = lax.broadcasted_iota(jnp.int32, (rows, kw), 1)
    col_minus_row = col - row
    r2 = lax.broadcasted_iota(jnp.int32, (SBA_SUM_PARTS * kw, 2 * kw), 0) % kw
    c2 = lax.broadcasted_iota(jnp.int32, (SBA_SUM_PARTS * kw, 2 * kw), 1)
    neg_cum = jnp.where((c2 >= kw) | (r2 > c2), -1.0, 0.0).astype(BF16)

    def sweep(ds, opening):
        grp = SBA_CUM_GROUP
        units = [(i, g0) for g0 in range(0, n_sub, grp) for i in range(len(ds))]
        last_i = len(ds) - 1
        zs, masks, sps, packed, sums, ws_b, kss, starts = {}, {}, {}, {}, {}, {}, {}, {}
        carries = {}
        wants = []

        def stage_scores(i, g0):
            for s in range(g0, g0 + grp):
                q0 = qi * qt + s * rows
                start = q0 + rows - (ds[i] + 1) * kw
                load = jnp.maximum(start, 0)
                starts[i, s] = start
                kss[i, s] = pl.ds(pl.multiple_of(load, rows), kw)
                zs[i, s] = lax.dot_general(q_ref[0, s * rows:(s + 1) * rows, :], k_ref[0, kss[i, s], :], NT_DIMS,
                                           preferred_element_type=F32)
                if opening and i == 0:
                    masks[i, s] = col_minus_row < (q0 - load)
                else:
                    masks[i, s] = col < (start + kw - load)

        def stage_softplus(i, g0):
            parts = []
            for s in range(g0, g0 + grp):
                sps[i, s] = jnp.where(masks[i, s], _softplus(zs[i, s]), 0.0)
                parts.append(jnp.concatenate(_split_bf16(sps[i, s], SBA_SUM_PARTS), axis=1))
            packed[i, g0] = jnp.concatenate(parts, axis=0)

        def stage_sums(i, g0):
            both = jnp.dot(packed.pop((i, g0)), neg_cum, preferred_element_type=F32)
            for j, s in enumerate(range(g0, g0 + grp)):
                sums[i, s] = both[j * rows:(j + 1) * rows]

        def stage_weights(i, g0):
            for s in range(g0, g0 + grp):
                if i == 0:
                    carries[s] = None if opening else c_ref[s]
                logw = zs[i, s] - sps[i, s] + sums[i, s][:, :kw]
                if carries[s] is not None:
                    logw = logw + carries[s]
                ws_b[i, s] = jnp.where(masks[i, s], jnp.exp(logw), 0.0).astype(BF16)
                tot = sums[i, s][:, kw:]
                carries[s] = tot if carries[s] is None else carries[s] + tot
                if i == last_i:
                    c_ref[s] = carries[s]
                    wants.append(jnp.where(starts[i, s] > 0, carries[s], -jnp.inf))

        def stage_values(i, g0):
            for s in range(g0, g0 + grp):
                pv = jnp.dot(ws_b[i, s], v_ref[0, kss[i, s], :], preferred_element_type=F32)
                if opening and i == 0:
                    acc_ref[s] = pv
                else:
                    acc_ref[s] += pv

        stages = (stage_scores, stage_softplus, stage_sums, stage_weights, stage_values)
        skew = SBA_STAGE_SKEW
        for tick in range(len(units) + (len(stages) - 1) * skew):
            for j in reversed(range(len(stages))):
                u = tick - j * skew
                if 0 <= u < len(units):
                    stages[j](*units[u])
        pending = wants[0]
        for want in wants[1:]:
            pending = jnp.maximum(pending, want)
        return jnp.max(pending)

    n_open = SBA_OPENING_STEPS
    first = sweep(list(range(n_open)), True)
    lax.while_loop(lambda st: st[1] > SBA_EXIT,
                   lambda st: (st[0] + 1, sweep([st[0]], False)),
                   (jnp.int32(n_open), first))
    for s in subs:
        qs = slice(s * rows, (s + 1) * rows)
        o_ref[qs, :] = (acc_ref[s] * _silu(g_ref[0, qs, :].astype(F32))).astype(o_ref.dtype)


def sba_group(p_sba):
    seq = p_sba.shape[1]
    qt = min(SBA_QT, seq)
    nh = SBA_HEADS
    return pl.pallas_call(
        _sba_kernel,
        grid=(nh, seq // qt),
        in_specs=[pl.BlockSpec((1, qt, SBA_DIM), lambda h, i: (h, i, 0)),
                  pl.BlockSpec((1, seq, SBA_DIM), lambda h, i: (nh + h, 0, 0)),
                  pl.BlockSpec((1, seq, SBA_DIM), lambda h, i: (2 * nh + h, 0, 0)),
                  pl.BlockSpec((1, qt, SBA_DIM), lambda h, i: (3 * nh + h, i, 0))],
        out_specs=pl.BlockSpec((qt, SBA_DIM), lambda h, i: (i, h)),
        out_shape=jax.ShapeDtypeStruct((seq, SBA_WIDTH), BF16),
        scratch_shapes=[pltpu.VMEM((qt // SBA_ROWS, SBA_ROWS, SBA_KEYS), F32),
                        pltpu.VMEM((qt // SBA_ROWS, SBA_ROWS, SBA_DIM), F32)],
        compiler_params=_cparams(("arbitrary", "arbitrary")),
        name="sba_group",
    )(p_sba, p_sba, p_sba, p_sba)


def _out_proj_kernel(ya_ref, yb_ref, wa_ref, wb_ref, x_ref, nw_ref, *out_refs, emit_x):
    acc = jnp.dot(ya_ref[...], wa_ref[...], preferred_element_type=F32)
    acc = acc + jnp.dot(yb_ref[...], wb_ref[...], preferred_element_type=F32)
    x_new = x_ref[...] + acc
    if emit_x:
        xo_ref, h_ref = out_refs
        xo_ref[...] = x_new
    else:
        (h_ref,) = out_refs
    h_ref[...] = _rmsnorm_rows(x_new, nw_ref[...]).astype(h_ref.dtype)


def out_proj(y_ssd, y_sba, w, x, norm_all, norm_idx, last, tm=256):
    seq, d = x.shape
    ka, kb = y_ssd.shape[1], y_sba.shape[1]
    assert ka == kb and w.shape == (ka + kb, d)
    row_spec = lambda width: pl.BlockSpec((tm, width), lambda i: (i, 0))
    if last:
        out_specs = row_spec(d)
        out_shape = jax.ShapeDtypeStruct((seq, d), F32)
    else:
        out_specs = [row_spec(d), row_spec(d)]
        out_shape = [jax.ShapeDtypeStruct((seq, d), F32), jax.ShapeDtypeStruct((seq, d), BF16)]
    return pl.pallas_call(
        functools.partial(_out_proj_kernel, emit_x=not last),
        grid=(seq // tm,),
        in_specs=[row_spec(ka), row_spec(kb),
                  pl.BlockSpec((ka, d), lambda i: (0, 0)),
                  pl.BlockSpec((kb, d), lambda i: (1, 0)),
                  row_spec(d),
                  pl.BlockSpec((None, 1, d), lambda i: (norm_idx, 0, 0))],
        out_specs=out_specs,
        out_shape=out_shape,
        compiler_params=_cparams(("arbitrary",)),
        name="out_proj",
    )(y_ssd, y_sba, w, w, x, norm_all)


def kernel(x, norm_w, w_in, conv_w, conv_b, dt_bias, a_log, d_skip, ssd_norm_w, w_out, final_norm_w):
    bsz, seq, d = x.shape
    assert bsz == 1 and d == D_MODEL
    depth = w_in.shape[0]
    sba_row0 = SSD_COLS + SSD_HEADS
    pad = LANES - SSD_HEADS
    w_in_t = jnp.swapaxes(w_in, 1, 2)
    dtb = jnp.pad(dt_bias, ((0, 0), (0, pad))).reshape(depth, 1, LANES)
    alog = jnp.pad(a_log, ((0, 0), (0, pad))).reshape(depth, 1, LANES)
    dsk = jnp.repeat(d_skip, SSD_HEAD_DIM, axis=1).reshape(depth, 1, SSD_WIDTH)
    conv_b3 = conv_b.reshape(depth, 1, CONV_DIM)
    ssd_nw = ssd_norm_w.reshape(depth, 1, SSD_WIDTH)
    norms = jnp.concatenate([norm_w, final_norm_w[None]], axis=0).reshape(depth + 1, 1, d)

    xs = x.reshape(seq, d)
    h = rmsnorm_bf16(xs, norm_w[0])
    for i in range(depth):
        p_ssd = in_proj(h, w_in_t, i, 0, SSD_COLS, False, "in_proj_ssd")
        p_sba, w_out_b = in_proj(h, w_in_t, i, sba_row0, 4 * SBA_WIDTH, True, "in_proj_sba",
                                 scaled_cols=SBA_WIDTH, scale=1.0 / math.sqrt(SBA_DIM), cast_along=w_out)
        y_ssd = ssd_group(p_ssd, h, i, w_in_t, conv_w, conv_b3, dtb, alog, dsk, ssd_nw)
        y_sba = sba_group(p_sba)
        last = i == depth - 1
        res = out_proj(y_ssd, y_sba, w_out_b, xs, norms, i + 1, last)
        if last:
            out = res
        else:
            xs, h = res
    return out.reshape(bsz, seq, d)
```

```python
import functools
import math

import jax
import jax.numpy as jnp
from jax import lax
from jax.experimental import pallas as pl
from jax.experimental.pallas import tpu as pltpu

F32 = jnp.float32
BF16 = jnp.bfloat16

D_MODEL = 2048
SSD_WIDTH = 2048
SSD_HEAD_DIM = 64
SSD_HEADS = SSD_WIDTH // SSD_HEAD_DIM
SSD_GROUPS = 4
SSD_STATE = 128
SSD_CONV = 4
SSD_BC = SSD_GROUPS * SSD_STATE
CONV_DIM = SSD_WIDTH + 2 * SSD_BC
SSD_COLS = SSD_WIDTH + CONV_DIM
SBA_HEADS = 16
SBA_DIM = 128
SBA_WIDTH = SBA_HEADS * SBA_DIM
EPS = 1e-6

LANES = 128
SUBLANES = 8
VMEM_LIMIT = 56 * 1024 * 1024

SSD_T = 128
CONV_BLK = 256
SSD_PAIR_SKEW = 4
SSD_CHAIN_TICKS = (1, 5, 10, 16)
SBA_ROWS = 64
SBA_KEYS = 128
SBA_CUM_GROUP = 2
SBA_SUM_PARTS = 1
SBA_OPENING_STEPS = 2
SBA_STAGE_SKEW = 4
SBA_QT = 8192
SBA_EXIT = -104.0


def _cparams(sem):
    return pltpu.CompilerParams(dimension_semantics=sem, vmem_limit_bytes=VMEM_LIMIT)


def _silu(a):
    half = 0.5 * a
    return half * jnp.tanh(half) + half


def _softplus(a):
    return jnp.maximum(a, 0.0) + jnp.log(1.0 + jnp.exp(-jnp.abs(a)))


def _split_bf16(a, parts):
    out = []
    r = a
    for i in range(parts):
        p = r.astype(BF16)
        out.append(p)
        if i + 1 < parts:
            r = r - p.astype(F32)
    return out


def _rmsnorm_rows(x, w):
    ms = jnp.mean(x * x, axis=-1, keepdims=True)
    return x * lax.rsqrt(ms + EPS) * w


def _rmsnorm_kernel(x_ref, w_ref, o_ref, *, rows):
    w = w_ref[...]

    def body(r, carry):
        sl = pl.ds(pl.multiple_of(r * rows, rows), rows)
        o_ref[sl, :] = _rmsnorm_rows(x_ref[sl, :], w).astype(o_ref.dtype)
        return carry

    lax.fori_loop(0, x_ref.shape[0] // rows, body, 0)


def rmsnorm_bf16(x, w, tm=1024):
    seq, d = x.shape
    return pl.pallas_call(
        functools.partial(_rmsnorm_kernel, rows=32),
        grid=(seq // tm,),
        in_specs=[pl.BlockSpec((tm, d), lambda i: (i, 0)),
                  pl.BlockSpec((1, d), lambda i: (0, 0))],
        out_specs=pl.BlockSpec((tm, d), lambda i: (i, 0)),
        out_shape=jax.ShapeDtypeStruct((seq, d), BF16),
        compiler_params=_cparams(("arbitrary",)),
        name="rmsnorm",
    )(x, w.reshape(1, d))


NT_DIMS = (((1,), (1,)), ((), ()))
W_ROW_ALIGN = 32


def _proj_kernel(h_ref, *refs, shift, heads, scaled_tiles, scale, side_cast):
    refs = list(refs)
    w_lo_ref = refs.pop(0)
    w_hi_ref = refs.pop(0) if shift else None
    side_in_ref = refs.pop(0) if side_cast else None
    o_ref = refs.pop(0)
    side_out_ref = refs.pop(0) if side_cast else None
    wb_ref = refs.pop(0)
    tn = wb_ref.shape[0]
    if side_cast:
        side_out_ref[...] = side_in_ref[...].astype(side_out_ref.dtype)

    @pl.when(pl.program_id(1) == 0)
    def _():
        rows = W_ROW_ALIGN
        factor = jnp.where(pl.program_id(0) < scaled_tiles, scale, 1.0).astype(F32) if scaled_tiles else None
        stage = (lambda w: (w * factor).astype(BF16)) if scaled_tiles else (lambda w: w.astype(BF16))

        def body(r, carry):
            dst = pl.ds(pl.multiple_of(r * rows, rows), rows)
            src = pl.ds(pl.multiple_of(r * rows + shift, rows), rows)
            wb_ref[dst, :] = stage(w_lo_ref[src, :])
            return carry

        lax.fori_loop(0, (tn - shift) // rows, body, 0)
        if shift:
            wb_ref[tn - shift:tn, :] = stage(w_hi_ref[...])

    acc = lax.dot_general(h_ref[...], wb_ref[...], NT_DIMS, preferred_element_type=F32)
    if heads:
        for hh in range(o_ref.shape[0]):
            o_ref[hh] = acc[:, hh * LANES:(hh + 1) * LANES].astype(o_ref.dtype)
    else:
        o_ref[...] = acc.astype(o_ref.dtype)


def in_proj(h, w_t, layer, row0, n, heads, name, scaled_cols=0, scale=1.0, cast_along=None, tm=1024, tn=1024):
    seq, k = h.shape
    shift = row0 % tn
    assert shift % W_ROW_ALIGN == 0 and n % tn == 0 and w_t.shape[1] >= row0 + n and scaled_cols % tn == 0
    lo0 = row0 // tn
    in_specs = [pl.BlockSpec((tm, k), lambda j, i: (i, 0)),
                pl.BlockSpec((None, tn, k), lambda j, i: (layer, lo0 + j, 0))]
    operands = [h, w_t]
    if shift:
        assert tn % shift == 0
        per = tn // shift
        in_specs.append(pl.BlockSpec((None, shift, k), lambda j, i: (layer, (lo0 + j + 1) * per, 0)))
        operands.append(w_t)
    m_tiles = seq // tm
    if cast_along is not None:
        steps = (n // tn) * m_tiles
        side_rows, side_cols = cast_along.shape[1:]
        assert side_rows % steps == 0 and (side_rows // steps) % W_ROW_ALIGN == 0
        in_specs.append(pl.BlockSpec((None, side_rows // steps, side_cols), lambda j, i: (layer, j * m_tiles + i, 0)))
        operands.append(cast_along)
    if heads:
        hp = tn // LANES
        out_spec = pl.BlockSpec((hp, tm, LANES), lambda j, i: (j, i, 0))
        out_shape = jax.ShapeDtypeStruct((n // LANES, seq, LANES), BF16)
    else:
        out_spec = pl.BlockSpec((tm, tn), lambda j, i: (i, j))
        out_shape = jax.ShapeDtypeStruct((seq, n), BF16)
    if cast_along is not None:
        out_spec = [out_spec, pl.BlockSpec((side_rows // steps, side_cols), lambda j, i: (j * m_tiles + i, 0))]
        out_shape = [out_shape, jax.ShapeDtypeStruct((side_rows, side_cols), BF16)]
    return pl.pallas_call(
        functools.partial(_proj_kernel, shift=shift, heads=heads, scaled_tiles=scaled_cols // tn, scale=scale,
                          side_cast=cast_along is not None),
        grid=(n // tn, m_tiles),
        in_specs=in_specs,
        out_specs=out_spec,
        out_shape=out_shape,
        scratch_shapes=[pltpu.VMEM((tn, k), BF16)],
        compiler_params=_cparams(("arbitrary", "arbitrary")),
        name=name,
    )(*operands)


def _ssd_kernel(p_ref, h0_ref, hn_ref, wdt_ref, cw_ref, cb_ref, dtb_ref, alog_ref, dsk_ref, nw_ref, o_ref,
                hbuf, xbc, state, ybuf, wdt_b, chain):
    t = SSD_T
    hist = SUBLANES
    c = pl.program_id(0)
    row = lax.broadcasted_iota(jnp.int32, (t, t), 0)
    col = lax.broadcasted_iota(jnp.int32, (t, t), 1)
    tril = row >= col
    tril_b = jnp.where(tril, 1.0, 0.0).astype(BF16)
    nxt = {}

    def chain_project(h_in):
        nxt["dt_raw"] = lax.dot_general(h_in[...], wdt_b[...], NT_DIMS, preferred_element_type=F32)

    def chain_cumsum():
        dt = _softplus(nxt["dt_raw"] + dtb_ref[...])
        da = dt * -jnp.exp(alog_ref[...])
        a_cum = None
        for part in _split_bf16(da, 3):
            term = jnp.dot(tril_b, part, preferred_element_type=F32)
            a_cum = term if a_cum is None else a_cum + term
        nxt["dt"], nxt["a_cum"] = dt, a_cum

    def chain_transpose():
        nxt["a_cum_t"] = nxt["a_cum"].T
        nxt["dt_t"] = nxt["dt"].T

    def chain_store():
        a_cum_t, dt_t = nxt["a_cum_t"], nxt["dt_t"]
        a_last_t = jnp.broadcast_to(a_cum_t[:, t - 1:t], a_cum_t.shape)
        chain[0] = nxt["a_cum"]
        chain[1] = a_cum_t
        chain[2] = dt_t
        chain[3] = jnp.exp(a_last_t - a_cum_t) * dt_t

    chain_stages = (chain_cumsum, chain_transpose, chain_store)

    @pl.when(c == 0)
    def _():
        hbuf[...] = jnp.zeros_like(hbuf)
        state[...] = jnp.zeros_like(state)
        wdt_b[0:SSD_HEADS, :] = wdt_ref[...].astype(BF16)
        wdt_b[SSD_HEADS:LANES, :] = jnp.zeros((LANES - SSD_HEADS, D_MODEL), BF16)
        chain_project(h0_ref)
        for stage in chain_stages:
            stage()

    a_cum = chain[0]
    a_cum_t = chain[1]
    dt_t = chain[2]
    w_t = chain[3]
    chunk_decay = jnp.exp(a_cum[t - 1:t, :])

    shift_by = {back: jnp.where(row - col == back, 1.0, 0.0).astype(BF16) for back in range(1, SSD_CONV)}
    top = lax.broadcasted_iota(jnp.int32, (hist, CONV_BLK), 0)

    def conv_blocks(j0, j1):
        for j in range(j0, j1):
            cs = slice(j * CONV_BLK, (j + 1) * CONV_BLK)
            ub = p_ref[:, SSD_WIDTH + j * CONV_BLK:SSD_WIDTH + (j + 1) * CONV_BLK]
            u = ub.astype(F32)
            prev = hbuf[:, cs]
            acc = cb_ref[:, cs] + cw_ref[SSD_CONV - 1:SSD_CONV, cs] * u
            for back in range(1, SSD_CONV):
                sh = jnp.dot(shift_by[back], ub, preferred_element_type=F32)
                head = jnp.where(top < back, pltpu.roll(prev, back, axis=0), sh[0:hist])
                sh = jnp.concatenate([head, sh[hist:]], axis=0)
                acc = acc + cw_ref[SSD_CONV - 1 - back:SSD_CONV - back, cs] * sh
            xbc[:, cs] = _silu(acc)
            hbuf[:, cs] = u[t - hist:t]

    conv_blocks(0, CONV_DIM // CONV_BLK)
    lane = lax.broadcasted_iota(jnp.int32, (t, LANES), 1)
    low_half = lane < SSD_HEAD_DIM

    heads_per_group = SSD_HEADS // SSD_GROUPS
    pairs_per_group = heads_per_group // 2
    n_pairs = SSD_HEADS // 2
    group_vals = {}
    prepared = {}
    issued = {}

    def pair_prepare(p):
        g = p // pairs_per_group
        if p % pairs_per_group == 0:
            b_g = xbc[:, SSD_WIDTH + g * SSD_STATE:SSD_WIDTH + (g + 1) * SSD_STATE]
            c_g = xbc[:, SSD_WIDTH + SSD_BC + g * SSD_STATE:SSD_WIDTH + SSD_BC + (g + 1) * SSD_STATE]
            c_gb = c_g.astype(BF16)
            cb = lax.dot_general(c_gb, b_g.astype(BF16), NT_DIMS, preferred_element_type=F32)
            group_vals[g] = (c_gb, cb, b_g.T)
        c_gb = group_vals[g][0]
        ps = slice(2 * p * SSD_HEAD_DIM, (2 * p + 2) * SSD_HEAD_DIM)
        x_pair = xbc[:, ps]
        x_lo = jnp.where(low_half, x_pair, 0.0).astype(BF16)
        x_hi = jnp.where(low_half, 0.0, x_pair).astype(BF16)
        off = jnp.dot(c_gb, state[:, ps].astype(BF16), preferred_element_type=F32)
        prepared[p] = (jnp.concatenate([x_lo, x_hi], axis=0), off, dsk_ref[:, ps] * x_pair)

    def pair_issue(p):
        _, cb, b_gt = group_vals[p // pairs_per_group]
        h0 = 2 * p
        x_cat, off, skip = prepared.pop(p)
        ms, wms, colvs = [], [], []
        for hh in (h0, h0 + 1):
            colv = jnp.broadcast_to(a_cum[:, hh:hh + 1], (t, t))
            rowv = a_cum_t[hh:hh + 1, :]
            decay = jnp.where(tril, jnp.exp(colv - rowv), 0.0)
            ms.append((cb * decay * dt_t[hh:hh + 1, :]).astype(BF16))
            wms.append((b_gt * w_t[hh:hh + 1, :]).astype(BF16))
            colvs.append(colv)
        yd = jnp.dot(jnp.concatenate(ms, axis=1), x_cat, preferred_element_type=F32)
        sn = jnp.dot(jnp.concatenate(wms, axis=1), x_cat, preferred_element_type=F32)
        e_pair = jnp.exp(jnp.where(low_half, colvs[0], colvs[1]))
        issued[p] = (yd, sn, e_pair, off, skip)

    def pair_finish(p):
        yd, sn, e_pair, off, skip = issued.pop(p)
        h0 = 2 * p
        ps = slice(h0 * SSD_HEAD_DIM, (h0 + 2) * SSD_HEAD_DIM)
        ybuf[:, ps] = yd + e_pair * off + skip
        cd = jnp.where(low_half[0:1, :], chunk_decay[:, h0:h0 + 1], chunk_decay[:, h0 + 1:h0 + 2])
        state[:, ps] = state[:, ps] * cd + sn

    def gate_group(g):
        gwid = SSD_WIDTH // SSD_GROUPS
        rows = 32
        gs = slice(g * gwid, (g + 1) * gwid)
        for r in range(t // rows):
            rs = slice(r * rows, (r + 1) * rows)
            yg = ybuf[rs, gs] * _silu(p_ref[rs, gs].astype(F32))
            ms = jnp.mean(yg * yg, axis=-1, keepdims=True)
            o_ref[rs, gs] = (yg * lax.rsqrt(ms + EPS) * nw_ref[:, gs]).astype(o_ref.dtype)

    for p in range(min(SSD_PAIR_SKEW, n_pairs)):
        pair_prepare(p)
    chain_ticks = {SSD_CHAIN_TICKS[0]: lambda: chain_project(hn_ref)}
    chain_ticks.update(zip(SSD_CHAIN_TICKS[1:], chain_stages))
    for tick in range(n_pairs + SSD_PAIR_SKEW):
        if tick in chain_ticks:
            chain_ticks[tick]()
        if tick + SSD_PAIR_SKEW < n_pairs:
            pair_prepare(tick + SSD_PAIR_SKEW)
        if tick < n_pairs:
            pair_issue(tick)
        done = tick - SSD_PAIR_SKEW
        if done >= 0:
            pair_finish(done)
            if done % pairs_per_group == pairs_per_group - 1:
                gate_group(done // pairs_per_group)


def ssd_group(p_ssd, h, layer, w_t, conv_w, conv_b, dtb, alog, dsk, norm_w):
    seq = p_ssd.shape[0]
    t = SSD_T
    assert SSD_COLS % SSD_HEADS == 0
    per_layer = lambda *shape: pl.BlockSpec((None,) + shape, lambda i: (layer,) + (0,) * len(shape))
    return pl.pallas_call(
        _ssd_kernel,
        grid=(seq // t,),
        in_specs=[pl.BlockSpec((t, SSD_COLS), lambda i: (i, 0)),
                  pl.BlockSpec((t, D_MODEL), lambda i: (0, 0)),
                  pl.BlockSpec((t, D_MODEL), lambda i: (jnp.minimum(i + 1, seq // t - 1), 0)),
                  pl.BlockSpec((None, SSD_HEADS, D_MODEL), lambda i: (layer, SSD_COLS // SSD_HEADS, 0)),
                  per_layer(SSD_CONV, CONV_DIM),
                  per_layer(1, CONV_DIM),
                  per_layer(1, LANES),
                  per_layer(1, LANES),
                  per_layer(1, SSD_WIDTH),
                  per_layer(1, SSD_WIDTH)],
        out_specs=pl.BlockSpec((t, SSD_WIDTH), lambda i: (i, 0)),
        out_shape=jax.ShapeDtypeStruct((seq, SSD_WIDTH), BF16),
        scratch_shapes=[pltpu.VMEM((SUBLANES, CONV_DIM), F32),
                        pltpu.VMEM((t, CONV_DIM), F32),
                        pltpu.VMEM((SSD_STATE, SSD_WIDTH), F32),
                        pltpu.VMEM((t, SSD_WIDTH), F32),
                        pltpu.VMEM((LANES, D_MODEL), BF16),
                        pltpu.VMEM((4, t, LANES), F32)],
        compiler_params=_cparams(("arbitrary",)),
        name="ssd_group",
    )(p_ssd, h, h, w_t, conv_w, conv_b, dtb, alog, dsk, norm_w)


def _sba_kernel(q_ref, k_ref, v_ref, g_ref, o_ref, c_ref, acc_ref):
    rows = SBA_ROWS
    kw = SBA_KEYS
    qt = q_ref.shape[1]
    n_sub = qt // rows
    subs = range(n_sub)
    qi = pl.program_id(1)
    row = lax.broadcasted_iota(jnp.int32, (rows, kw), 0)
    col = lax.broadcasted_iota(jnp.int32, (rows, kw), 1)
    col_minus_row = col - row
    r2 = lax.broadcasted_iota(jnp.int32, (SBA_SUM_PARTS * kw, 2 * kw), 0) % kw
    c2 = lax.broadcasted_iota(jnp.int32, (SBA_SUM_PARTS * kw, 2 * kw), 1)
    neg_cum = jnp.where((c2 >= kw) | (r2 > c2), -1.0, 0.0).astype(BF16)

    def sweep(ds, opening):
        grp = SBA_CUM_GROUP
        units = [(i, g0) for g0 in range(0, n_sub, grp) for i in range(len(ds))]
        last_i = len(ds) - 1
        zs, masks, sps, packed, sums, ws_b, kss, starts = {}, {}, {}, {}, {}, {}, {}, {}
        carries = {}
        wants = []

        def stage_scores(i, g0):
            for s in range(g0, g0 + grp):
                q0 = qi * qt + s * rows
                start = q0 + rows - (ds[i] + 1) * kw
                load = jnp.maximum(start, 0)
                starts[i, s] = start
                kss[i, s] = pl.ds(pl.multiple_of(load, rows), kw)
                zs[i, s] = lax.dot_general(q_ref[0, s * rows:(s + 1) * rows, :], k_ref[0, kss[i, s], :], NT_DIMS,
                                           preferred_element_type=F32)
                if opening and i == 0:
                    masks[i, s] = col_minus_row < (q0 - load)
                else:
                    masks[i, s] = col < (start + kw - load)

        def stage_softplus(i, g0):
            parts = []
            for s in range(g0, g0 + grp):
                sps[i, s] = jnp.where(masks[i, s], _softplus(zs[i, s]), 0.0)
                parts.append(jnp.concatenate(_split_bf16(sps[i, s], SBA_SUM_PARTS), axis=1))
            packed[i, g0] = jnp.concatenate(parts, axis=0)

        def stage_sums(i, g0):
            both = jnp.dot(packed.pop((i, g0)), neg_cum, preferred_element_type=F32)
            for j, s in enumerate(range(g0, g0 + grp)):
                sums[i, s] = both[j * rows:(j + 1) * rows]

        def stage_weights(i, g0):
            for s in range(g0, g0 + grp):
                if i == 0:
                    carries[s] = None if opening else c_ref[s]
                logw = zs[i, s] - sps[i, s] + sums[i, s][:, :kw]
                if carries[s] is not None:
                    logw = logw + carries[s]
                ws_b[i, s] = jnp.where(masks[i, s], jnp.exp(logw), 0.0).astype(BF16)
                tot = sums[i, s][:, kw:]
                carries[s] = tot if carries[s] is None else carries[s] + tot
                if i == last_i:
                    c_ref[s] = carries[s]
                    wants.append(jnp.where(starts[i, s] > 0, carries[s], -jnp.inf))

        def stage_values(i, g0):
            for s in range(g0, g0 + grp):
                pv = jnp.dot(ws_b[i, s], v_ref[0, kss[i, s], :], preferred_element_type=F32)
                if opening and i == 0:
                    acc_ref[s] = pv
                else:
                    acc_ref[s] += pv

        stages = (stage_scores, stage_softplus, stage_sums, stage_weights, stage_values)
        skew = SBA_STAGE_SKEW
        for tick in range(len(units) + (len(stages) - 1) * skew):
            for j in reversed(range(len(stages))):
                u = tick - j * skew
                if 0 <= u < len(units):
                    stages[j](*units[u])
        pending = wants[0]
        for want in wants[1:]:
            pending = jnp.maximum(pending, want)
        return jnp.max(pending)

    n_open = SBA_OPENING_STEPS
    first = sweep(list(range(n_open)), True)
    lax.while_loop(lambda st: st[1] > SBA_EXIT,
                   lambda st: (st[0] + 1, sweep([st[0]], False)),
                   (jnp.int32(n_open), first))
    for s in subs:
        qs = slice(s * rows, (s + 1) * rows)
        o_ref[qs, :] = (acc_ref[s] * _silu(g_ref[0, qs, :].astype(F32))).astype(o_ref.dtype)


def sba_group(p_sba):
    seq = p_sba.shape[1]
    qt = min(SBA_QT, seq)
    nh = SBA_HEADS
    return pl.pallas_call(
        _sba_kernel,
        grid=(nh, seq // qt),
        in_specs=[pl.BlockSpec((1, qt, SBA_DIM), lambda h, i: (h, i, 0)),
                  pl.BlockSpec((1, seq, SBA_DIM), lambda h, i: (nh + h, 0, 0)),
                  pl.BlockSpec((1, seq, SBA_DIM), lambda h, i: (2 * nh + h, 0, 0)),
                  pl.BlockSpec((1, qt, SBA_DIM), lambda h, i: (3 * nh + h, i, 0))],
        out_specs=pl.BlockSpec((qt, SBA_DIM), lambda h, i: (i, h)),
        out_shape=jax.ShapeDtypeStruct((seq, SBA_WIDTH), BF16),
        scratch_shapes=[pltpu.VMEM((qt // SBA_ROWS, SBA_ROWS, SBA_KEYS), F32),
                        pltpu.VMEM((qt // SBA_ROWS, SBA_ROWS, SBA_DIM), F32)],
        compiler_params=_cparams(("arbitrary", "arbitrary")),
        name="sba_group",
    )(p_sba, p_sba, p_sba, p_sba)


def _out_proj_kernel(ya_ref, yb_ref, wa_ref, wb_ref, x_ref, nw_ref, *out_refs, emit_x):
    acc = jnp.dot(ya_ref[...], wa_ref[...], preferred_element_type=F32)
    acc = acc + jnp.dot(yb_ref[...], wb_ref[...], preferred_element_type=F32)
    x_new = x_ref[...] + acc
    if emit_x:
        xo_ref, h_ref = out_refs
        xo_ref[...] = x_new
    else:
        (h_ref,) = out_refs
    h_ref[...] = _rmsnorm_rows(x_new, nw_ref[...]).astype(h_ref.dtype)


def out_proj(y_ssd, y_sba, w, x, norm_all, norm_idx, last, tm=256):
    seq, d = x.shape
    ka, kb = y_ssd.shape[1], y_sba.shape[1]
    assert ka == kb and w.shape == (ka + kb, d)
    row_spec = lambda width: pl.BlockSpec((tm, width), lambda i: (i, 0))
    if last:
        out_specs = row_spec(d)
        out_shape = jax.ShapeDtypeStruct((seq, d), F32)
    else:
        out_specs = [row_spec(d), row_spec(d)]
        out_shape = [jax.ShapeDtypeStruct((seq, d), F32), jax.ShapeDtypeStruct((seq, d), BF16)]
    return pl.pallas_call(
        functools.partial(_out_proj_kernel, emit_x=not last),
        grid=(seq // tm,),
        in_specs=[row_spec(ka), row_spec(kb),
                  pl.BlockSpec((ka, d), lambda i: (0, 0)),
                  pl.BlockSpec((kb, d), lambda i: (1, 0)),
                  row_spec(d),
                  pl.BlockSpec((None, 1, d), lambda i: (norm_idx, 0, 0))],
        out_specs=out_specs,
        out_shape=out_shape,
        compiler_params=_cparams(("arbitrary",)),
        name="out_proj",
    )(y_ssd, y_sba, w, w, x, norm_all)


def kernel(x, norm_w, w_in, conv_w, conv_b, dt_bias, a_log, d_skip, ssd_norm_w, w_out, final_norm_w):
    bsz, seq, d = x.shape
    assert bsz == 1 and d == D_MODEL
    depth = w_in.shape[0]
    sba_row0 = SSD_COLS + SSD_HEADS
    pad = LANES - SSD_HEADS
    w_in_t = jnp.swapaxes(w_in, 1, 2)
    dtb = jnp.pad(dt_bias, ((0, 0), (0, pad))).reshape(depth, 1, LANES)
    alog = jnp.pad(a_log, ((0, 0), (0, pad))).reshape(depth, 1, LANES)
    dsk = jnp.repeat(d_skip, SSD_HEAD_DIM, axis=1).reshape(depth, 1, SSD_WIDTH)
    conv_b3 = conv_b.reshape(depth, 1, CONV_DIM)
    ssd_nw = ssd_norm_w.reshape(depth, 1, SSD_WIDTH)
    norms = jnp.concatenate([norm_w, final_norm_w[None]], axis=0).reshape(depth + 1, 1, d)

    xs = x.reshape(seq, d)
    h = rmsnorm_bf16(xs, norm_w[0])
    for i in range(depth):
        p_ssd = in_proj(h, w_in_t, i, 0, SSD_COLS, False, "in_proj_ssd")
        p_sba, w_out_b = in_proj(h, w_in_t, i, sba_row0, 4 * SBA_WIDTH, True, "in_proj_sba",
                                 scaled_cols=SBA_WIDTH, scale=1.0 / math.sqrt(SBA_DIM), cast_along=w_out)
        y_ssd = ssd_group(p_ssd, h, i, w_in_t, conv_w, conv_b3, dtb, alog, dsk, ssd_nw)
        y_sba = sba_group(p_sba)
        last = i == depth - 1
        res = out_proj(y_ssd, y_sba, w_out_b, xs, norms, i + 1, last)
        if last:
            out = res
        else:
            xs, h = res
    return out.reshape(bsz, seq, d)
```
